```python
import math
import jax, jax.numpy as jnp
from jax import lax
import numpy as np

D_MODEL = 2048
BATCH = 4
SEQ = 8192
DEPTH = 1

D_RNN = 2048
RNN_HEADS = 16
RNN_BLOCK = D_RNN // RNN_HEADS
CONV_WIDTH = 4
LRU_C = 8.0
MLA_HEADS = 16
Q_LORA = 512
KV_LORA = 512
QK_NOPE = 128
QK_ROPE = 64
V_HEAD = 128
ROPE_THETA = 10000.0
Q_BLOCK = 128
PEER_HEADS = 8
N_KEYS = 128
N_EXPERTS = N_KEYS * N_KEYS
D_KEY = 256
PEER_TOPK = 16
PEER_CHUNK = 128
DN_ALPHA = (2.0 * DEPTH) ** 0.25
DN_BETA = (8.0 * DEPTH) ** -0.25
LN_EPS = 1e-5
RMS_EPS = 1e-6
IN_WIDTHS = (D_RNN, D_RNN, Q_LORA, KV_LORA, QK_ROPE, D_MODEL, D_MODEL)
IN_COLS = sum(IN_WIDTHS)
IN_SPLITS = tuple(int(v) for v in np.cumsum(IN_WIDTHS)[:-1])

kernel_name = "hybrid_rglru_mla_peer_deepnorm_adaln"


def layer_norm(x, g, b):
    xf = x.astype(jnp.float32)
    mu = jnp.mean(xf, -1, keepdims=True)
    var = jnp.mean(jnp.square(xf - mu), -1, keepdims=True)
    return ((xf - mu) * lax.rsqrt(var + LN_EPS)).astype(x.dtype) * g + b


def rms_norm(x, g):
    xf = x.astype(jnp.float32)
    return (xf * lax.rsqrt(jnp.mean(xf * xf, -1, keepdims=True) + RMS_EPS)).astype(x.dtype) * g


def apply_rope(x, cos, sin):
    x1, x2 = jnp.split(x, 2, axis=-1)
    return jnp.concatenate([x1 * cos - x2 * sin, x2 * cos + x1 * sin], axis=-1)


def rg_lru_branch(xr, gr, conv_w, conv_b, w_a, b_a, w_x, b_x, lam):
    B, S, _ = xr.shape
    xc = lax.conv_general_dilated(
        xr, conv_w[:, None, :], window_strides=(1,), padding=[(CONV_WIDTH - 1, 0)],
        dimension_numbers=('NWC', 'WIO', 'NWC'), feature_group_count=D_RNN) + conv_b
    xh = xc.reshape(B, S, RNN_HEADS, RNN_BLOCK)
    r = jax.nn.sigmoid(jnp.einsum('bshi,hij->bshj', xh, w_a).reshape(B, S, D_RNN) + b_a)
    i = jax.nn.sigmoid(jnp.einsum('bshi,hij->bshj', xh, w_x).reshape(B, S, D_RNN) + b_x)
    log_a = (-LRU_C * r.astype(jnp.float32)) * jax.nn.softplus(-lam.astype(jnp.float32))
    a = jnp.exp(log_a)
    mult = jnp.sqrt(-jnp.expm1(2.0 * log_a))
    bx = mult * (i * xc).astype(jnp.float32)

    def combine(left, right):
        a1, b1 = left
        a2, b2 = right
        return a1 * a2, a2 * b1 + b2

    _, h = lax.associative_scan(combine, (a, bx), axis=1)
    return h.astype(xr.dtype) * jax.nn.gelu(gr)


def mla_branch(q_c, kv_c, k_r, cos, sin, q_norm_g, w_uq, kv_norm_g, w_ukv):
    B, S, _ = q_c.shape
    q = (rms_norm(q_c, q_norm_g) @ w_uq).reshape(B, S, MLA_HEADS, QK_NOPE + QK_ROPE)
    q_nope = q[..., :QK_NOPE]
    q_rope = apply_rope(q[..., QK_NOPE:], cos[:, :, None, :], sin[:, :, None, :])
    kv = (rms_norm(kv_c, kv_norm_g) @ w_ukv).reshape(B, S, MLA_HEADS, QK_NOPE + V_HEAD)
    k_nope = kv[..., :QK_NOPE]
    v = kv[..., QK_NOPE:]
    k_rope = apply_rope(k_r, cos, sin)
    scale = (QK_NOPE + QK_ROPE) ** -0.5
    nb = S // Q_BLOCK
    qn_blocks = q_nope.reshape(B, nb, Q_BLOCK, MLA_HEADS, QK_NOPE).transpose(1, 0, 2, 3, 4)
    qr_blocks = q_rope.reshape(B, nb, Q_BLOCK, MLA_HEADS, QK_ROPE).transpose(1, 0, 2, 3, 4)
    key_pos = jnp.arange(S)

    def attend(args):
        qn, qr, blk = args
        s = (jnp.einsum('bqhd,bkhd->bhqk', qn, k_nope)
             + jnp.einsum('bqhr,bkr->bhqk', qr, k_rope)).astype(jnp.float32) * scale
        q_pos = blk * Q_BLOCK + jnp.arange(Q_BLOCK)
        causal = key_pos[None, :] <= q_pos[:, None]
        s = jnp.where(causal, s, -jnp.inf)
        p = jax.nn.softmax(s, axis=-1).astype(v.dtype)
        return jnp.einsum('bhqk,bkhd->bqhd', p, v)

    o = lax.map(attend, (qn_blocks, qr_blocks, jnp.arange(nb)))
    return o.transpose(1, 0, 2, 3, 4).reshape(B, S, MLA_HEADS * V_HEAD)


def peer_ffn(u, w_q, sub_keys, expert_u, expert_v):
    B, S, D = u.shape
    T = B * S
    ut = u.reshape(T, D)
    q = (ut @ w_q).reshape(T, PEER_HEADS, 2, D_KEY // 2)
    s = jnp.einsum('thpd,pnd->thpn', q, sub_keys).astype(jnp.float32)
    top_s, top_i = lax.top_k(s, PEER_TOPK)
    cand_s = top_s[:, :, 0, :, None] + top_s[:, :, 1, None, :]
    cand_i = top_i[:, :, 0, :, None] * N_KEYS + top_i[:, :, 1, None, :]
    best_s, best_j = lax.top_k(cand_s.reshape(T, PEER_HEADS, PEER_TOPK * PEER_TOPK), PEER_TOPK)
    idx = jnp.take_along_axis(cand_i.reshape(T, PEER_HEADS, PEER_TOPK * PEER_TOPK), best_j, axis=-1)
    g = jax.nn.softmax(best_s, axis=-1).astype(u.dtype)
    nc = T // PEER_CHUNK

    def experts(args):
        xc, ic, gc = args
        act = jax.nn.gelu(jnp.einsum('cd,chkd->chk', xc, jnp.take(expert_u, ic, axis=0))) * gc
        return jnp.einsum('chk,chkd->cd', act, jnp.take(expert_v, ic, axis=0))

    y = lax.map(experts, (ut.reshape(nc, PEER_CHUNK, D),
                          idx.reshape(nc, PEER_CHUNK, PEER_HEADS, PEER_TOPK),
                          g.reshape(nc, PEER_CHUNK, PEER_HEADS, PEER_TOPK)))
    return y.reshape(B, S, D)


def setup_inputs(seed: int = 0) -> dict:
    key = jax.random.key(seed)
    ks = jax.random.split(key, 32)
    L, D = DEPTH, D_MODEL
    f32 = jnp.float32

    def nrm(k, shape, scale):
        return jax.random.normal(k, shape, f32) * scale

    x = jax.random.normal(ks[0], (BATCH, SEQ, D), f32)
    c = jax.random.normal(ks[1], (BATCH, D), f32)
    offs = jax.random.randint(ks[2], (BATCH, 1), 0, 1024, dtype=jnp.int32)
    positions = (offs + jnp.arange(SEQ, dtype=jnp.int32)[None, :]).astype(jnp.int32)
    a0 = jax.random.uniform(ks[3], (L, D_RNN), f32, 0.9, 0.999)
    p0 = a0 ** (1.0 / LRU_C)
    rg_lambda = jnp.log(p0) - jnp.log1p(-p0)
    return {
        "x": x,
        "c": c,
        "positions": positions,
        "w_ada": nrm(ks[4], (L, D, 6 * D), 0.1 * D ** -0.5),
        "b_ada": nrm(ks[5], (L, 6 * D), 0.01),
        "w_in": nrm(ks[6], (L, D, IN_COLS), D ** -0.5),
        "conv_w": nrm(ks[7], (L, CONV_WIDTH, D_RNN), CONV_WIDTH ** -0.5),
        "conv_b": nrm(ks[8], (L, D_RNN), 0.01),
        "w_rg_a": nrm(ks[9], (L, RNN_HEADS, RNN_BLOCK, RNN_BLOCK), RNN_BLOCK ** -0.5),
        "b_rg_a": nrm(ks[10], (L, D_RNN), 0.01),
        "w_rg_x": nrm(ks[11], (L, RNN_HEADS, RNN_BLOCK, RNN_BLOCK), RNN_BLOCK ** -0.5),
        "b_rg_x": nrm(ks[12], (L, D_RNN), 0.01),
        "rg_lambda": rg_lambda,
        "w_rnn_out": nrm(ks[13], (L, D_RNN, D), DN_BETA * D_RNN ** -0.5),
        "q_norm_g": 1.0 + nrm(ks[14], (L, Q_LORA), 0.01),
        "w_uq": nrm(ks[15], (L, Q_LORA, MLA_HEADS * (QK_NOPE + QK_ROPE)), Q_LORA ** -0.5),
        "kv_norm_g": 1.0 + nrm(ks[16], (L, KV_LORA), 0.01),
        "w_ukv": nrm(ks[17], (L, KV_LORA, MLA_HEADS * (QK_NOPE + V_HEAD)), KV_LORA ** -0.5),
        "w_mla_out": nrm(ks[18], (L, MLA_HEADS * V_HEAD, D), DN_BETA * (MLA_HEADS * V_HEAD) ** -0.5),
        "w_o": nrm(ks[19], (L, D, D), DN_BETA * D ** -0.5),
        "ln1_g": 1.0 + nrm(ks[20], (L, D), 0.01),
        "ln1_b": nrm(ks[21], (L, D), 0.01),
        "peer_wq": nrm(ks[22], (L, D, PEER_HEADS * D_KEY), D ** -0.5),
        "peer_keys": nrm(ks[23], (L, 2, N_KEYS, D_KEY // 2), (D_KEY // 2) ** -0.5),
        "peer_u": nrm(ks[24], (L, N_EXPERTS, D), D ** -0.5),
        "peer_v": nrm(ks[25], (L, N_EXPERTS, D), DN_BETA),
        "ln2_g": 1.0 + nrm(ks[26], (L, D), 0.01),
        "ln2_b": nrm(ks[27], (L, D), 0.01),
    }


def reference(x, c, positions, w_ada, b_ada, w_in, conv_w, conv_b, w_rg_a, b_rg_a, w_rg_x, b_rg_x,
              rg_lambda, w_rnn_out, q_norm_g, w_uq, kv_norm_g, w_ukv, w_mla_out, w_o, ln1_g, ln1_b,
              peer_wq, peer_keys, peer_u, peer_v, ln2_g, ln2_b):
    inv_freq = ROPE_THETA ** (-jnp.arange(0, QK_ROPE, 2, dtype=jnp.float32) / QK_ROPE)
    ang = positions.astype(jnp.float32)[..., None] * inv_freq
    cos = jnp.cos(ang).astype(x.dtype)
    sin = jnp.sin(ang).astype(x.dtype)
    c_act = jax.nn.silu(c)
    for l in range(DEPTH):
        mod = c_act @ w_ada[l] + b_ada[l]
        sh1, sc1, gt1, sh2, sc2, gt2 = [m[:, None, :] for m in jnp.split(mod, 6, axis=-1)]

        h = x * (1.0 + sc1) + sh1
        proj = h @ w_in[l]
        xr, gr, q_c, kv_c, k_r, g_rnn, g_mla = jnp.split(proj, IN_SPLITS, axis=-1)
        y_rnn = rg_lru_branch(xr, gr, conv_w[l], conv_b[l], w_rg_a[l], b_rg_a[l],
                              w_rg_x[l], b_rg_x[l], rg_lambda[l]) @ w_rnn_out[l]
        y_mla = mla_branch(q_c, kv_c, k_r, cos, sin, q_norm_g[l], w_uq[l],
                           kv_norm_g[l], w_ukv[l]) @ w_mla_out[l]
        mixed = (jax.nn.sigmoid(g_rnn) * y_rnn + jax.nn.sigmoid(g_mla) * y_mla) @ w_o[l]
        x = layer_norm(DN_ALPHA * x + (1.0 + gt1) * mixed, ln1_g[l], ln1_b[l])

        h = x * (1.0 + sc2) + sh2
        y_ffn = peer_ffn(h, peer_wq[l], peer_keys[l], peer_u[l], peer_v[l])
        x = layer_norm(DN_ALPHA * x + (1.0 + gt2) * y_ffn, ln2_g[l], ln2_b[l])
    return x
```

```python
import functools
import math

import jax
import jax.numpy as jnp
from jax import lax
from jax.experimental import pallas as pl
from jax.experimental.pallas import tpu as pltpu

F32 = jnp.float32
BF16 = jnp.bfloat16

CONV_WIDTH = 4
LRU_C = 8.0
QK_NOPE = 128
QK_ROPE = 64
V_HEAD = 128
ROPE_THETA = 10000.0
PEER_TOPK = 16
LN_EPS = 1e-5
RMS_EPS = 1e-6
LANES = 128
SUBLANES = 8
VMEM_LIMIT = 56 * 1024 * 1024
NEG_INF = float("-inf")


def _params(sem):
    return pltpu.CompilerParams(dimension_semantics=sem, vmem_limit_bytes=VMEM_LIMIT)


def _resident(shape, index_map):
    return pl.BlockSpec(shape, index_map, pipeline_mode=pl.Buffered(1))


def _gelu_tanh(x):
    return 0.5 * x * (1.0 + jnp.tanh(math.sqrt(2.0 / math.pi) * (x + 0.044715 * (x * x * x))))


def _layer_norm(z, g, b):
    mu = jnp.mean(z, axis=-1, keepdims=True)
    d = z - mu
    var = jnp.mean(d * d, axis=-1, keepdims=True)
    return d * lax.rsqrt(var + LN_EPS) * g + b


def _ada_kernel(c_ref, w_ref, b_ref, o_ref):
    c = c_ref[...]
    ca = c * jax.nn.sigmoid(c)
    o_ref[...] = jnp.dot(ca, w_ref[...], preferred_element_type=F32,
                         precision=lax.Precision.HIGHEST) + b_ref[...]


def _ada(c_pad, w, b):
    m, d = c_pad.shape
    n = w.shape[1]
    tn = min(n, 1024)
    return pl.pallas_call(
        _ada_kernel,
        grid=(n // tn,),
        in_specs=[pl.BlockSpec((m, d), lambda j: (0, 0)),
                  pl.BlockSpec((d, tn), lambda j: (0, j)),
                  pl.BlockSpec((1, tn), lambda j: (0, j))],
        out_specs=pl.BlockSpec((m, tn), lambda j: (0, j)),
        out_shape=jax.ShapeDtypeStruct((m, n), F32),
        compiler_params=_params(("arbitrary",)),
        name="ada",
    )(c_pad, w, b)


def _proj_kernel(x_ref, sc_ref, sh_ref, w_ref, o_ref, a_scr):
    @pl.when(pl.program_id(1) == 0)
    def _():
        a_scr[...] = (x_ref[...] * (1.0 + sc_ref[0]) + sh_ref[0]).astype(BF16)

    o_ref[...] = jnp.dot(a_scr[...], w_ref[...], preferred_element_type=F32)


def _proj(x2, sc, sh, w, seq):
    t, d = x2.shape
    n = w.shape[1]
    tm = min(1024, seq)
    tn = 512
    per_b = seq // tm
    return pl.pallas_call(
        _proj_kernel,
        grid=(t // tm, n // tn),
        in_specs=[pl.BlockSpec((tm, d), lambda i, j: (i, 0)),
                  pl.BlockSpec((1, 1, d), lambda i, j: (i // per_b, 0, 0)),
                  pl.BlockSpec((1, 1, d), lambda i, j: (i // per_b, 0, 0)),
                  pl.BlockSpec((d, tn), lambda i, j: (0, j))],
        out_specs=pl.BlockSpec((tm, tn), lambda i, j: (i, j)),
        out_shape=jax.ShapeDtypeStruct((t, n), F32),
        scratch_shapes=[pltpu.VMEM((tm, d), BF16)],
        compiler_params=_params(("parallel", "arbitrary")),
        name="proj",
    )(x2, sc, sh, w)


def _rglru_kernel(xr_ref, gr_ref, cw_ref, cb_ref, wa_ref, ba_ref, wx_ref, bx_ref, lam_ref,
                  o_ref, xbuf, abuf, bbuf, h_scr):
    ts, tc = xr_ref.shape
    pad = SUBLANES

    @pl.when(pl.program_id(2) == 0)
    def _():
        xbuf[0:pad, :] = jnp.zeros((pad, tc), F32)
        h_scr[...] = jnp.zeros_like(h_scr)

    xbuf[pad:pad + ts, :] = xr_ref[...]
    xc = cb_ref[...] + cw_ref[0:1, :] * xbuf[pad - 3:pad - 3 + ts, :]
    for k in range(1, CONV_WIDTH):
        off = pad - (CONV_WIDTH - 1) + k
        xc = xc + cw_ref[k:k + 1, :] * xbuf[off:off + ts, :]
    xbuf[0:pad, :] = xbuf[ts:ts + pad, :]

    xcb = xc.astype(BF16)
    ra, ia = [], []
    for hh in range(tc // LANES):
        xh = xcb[:, hh * LANES:(hh + 1) * LANES]
        ra.append(jnp.dot(xh, wa_ref[hh], preferred_element_type=F32))
        ia.append(jnp.dot(xh, wx_ref[hh], preferred_element_type=F32))
    r = jax.nn.sigmoid(jnp.concatenate(ra, axis=1) + ba_ref[...])
    ig = jax.nn.sigmoid(jnp.concatenate(ia, axis=1) + bx_ref[...])
    nlam = -lam_ref[...]
    softplus = jnp.maximum(nlam, 0.0) + jnp.log1p(jnp.exp(-jnp.abs(nlam)))
    log_a = (-LRU_C * r) * softplus
    a = jnp.exp(log_a)
    b = jnp.sqrt(-jnp.tanh(log_a) * (a * a + 1.0)) * (ig * xc)

    abuf[0:ts, :] = jnp.ones((ts, tc), F32)
    bbuf[0:ts, :] = jnp.zeros((ts, tc), F32)
    d = 1
    while d < ts:
        abuf[ts:2 * ts, :] = a
        bbuf[ts:2 * ts, :] = b
        a_sh = abuf[ts - d:2 * ts - d, :]
        b_sh = bbuf[ts - d:2 * ts - d, :]
        b = a * b_sh + b
        a = a * a_sh
        d *= 2
    h = a * h_scr[...] + b
    h_scr[...] = h[ts - 1:ts, :]
    o_ref[...] = (h * _gelu_tanh(gr_ref[...])).astype(o_ref.dtype)


def _rglru(proj3, col_xr, col_gr, conv_w, conv_b, wa, ba, wx, bx, lam):
    bsz, seq, _ = proj3.shape
    d_rnn = conv_w.shape[1]
    tc = min(512, d_rnn)
    ts = min(512, seq)
    nc = d_rnn // tc
    hpt = tc // LANES
    vec = lambda: pl.BlockSpec((1, tc), lambda b, c, s: (0, c))
    return pl.pallas_call(
        _rglru_kernel,
        grid=(bsz, nc, seq // ts),
        in_specs=[pl.BlockSpec((None, ts, tc), lambda b, c, s: (b, s, col_xr // tc + c)),
                  pl.BlockSpec((None, ts, tc), lambda b, c, s: (b, s, col_gr // tc + c)),
                  pl.BlockSpec((CONV_WIDTH, tc), lambda b, c, s: (0, c)),
                  vec(),
                  pl.BlockSpec((hpt, LANES, LANES), lambda b, c, s: (c, 0, 0)),
                  vec(),
                  pl.BlockSpec((hpt, LANES, LANES), lambda b, c, s: (c, 0, 0)),
                  vec(), vec()],
        out_specs=pl.BlockSpec((None, ts, tc), lambda b, c, s: (b, s, c)),
        out_shape=jax.ShapeDtypeStruct((bsz, seq, d_rnn), BF16),
        scratch_shapes=[pltpu.VMEM((ts + 2 * SUBLANES, tc), F32),
                        pltpu.VMEM((2 * ts, tc), F32),
                        pltpu.VMEM((2 * ts, tc), F32),
                        pltpu.VMEM((1, tc), F32)],
        compiler_params=_params(("parallel", "parallel", "arbitrary")),
        name="rglru",
    )(proj3, proj3, conv_w, conv_b, wa, ba, wx, bx, lam)


def _rope_kernel(pos_ref, invf_ref, cm_ref, sm_ref, cc_ref, ss_ref):
    ang = pos_ref[...].astype(F32) * invf_ref[...]
    cc_ref[...] = jnp.cos(ang) * cm_ref[...]
    ss_ref[...] = jnp.sin(ang) * sm_ref[...]


def _rope_tables(pos_col, invf, cmask, smask):
    t = pos_col.shape[0]
    tm = min(1024, t)
    row = lambda: pl.BlockSpec((1, LANES), lambda i: (0, 0))
    return pl.pallas_call(
        _rope_kernel,
        grid=(t // tm,),
        in_specs=[pl.BlockSpec((tm, 1), lambda i: (i, 0)), row(), row(), row()],
        out_specs=[pl.BlockSpec((tm, LANES), lambda i: (i, 0))] * 2,
        out_shape=[jax.ShapeDtypeStruct((t, LANES), F32)] * 2,
        compiler_params=_params(("parallel",)),
        name="rope",
    )(pos_col, invf, cmask, smask)


def _rms_bf16(x, g):
    return (x * lax.rsqrt(jnp.mean(x * x, axis=-1, keepdims=True) + RMS_EPS) * g).astype(BF16)


def _rope_lanes(r, cc, ss):
    return r * cc + pltpu.roll(r, LANES // 2, axis=1) * ss


def _qproj_kernel(qc_ref, g_ref, w_ref, cc_ref, ss_ref, o_ref, a_scr):
    @pl.when(pl.program_id(1) == 0)
    def _():
        a_scr[...] = _rms_bf16(qc_ref[...], g_ref[...])

    acc = jnp.dot(a_scr[...], w_ref[...], preferred_element_type=F32)
    o_ref[:, 0:QK_NOPE] = acc[:, 0:QK_NOPE].astype(o_ref.dtype)
    o_ref[:, QK_NOPE:] = _rope_lanes(acc[:, QK_NOPE:], cc_ref[...], ss_ref[...]).astype(o_ref.dtype)


def _qproj(proj, col_qc, g, w_ext, cc, ss, bsz, seq):
    t = proj.shape[0]
    nh, k, n = w_ext.shape
    tm = min(1024, seq)
    per_b = seq // tm
    return pl.pallas_call(
        _qproj_kernel,
        grid=(t // tm, nh),
        in_specs=[pl.BlockSpec((tm, k), lambda i, h: (i, col_qc // k)),
                  pl.BlockSpec((1, k), lambda i, h: (0, 0)),
                  pl.BlockSpec((None, k, n), lambda i, h: (h, 0, 0)),
                  pl.BlockSpec((tm, LANES), lambda i, h: (i, 0)),
                  pl.BlockSpec((tm, LANES), lambda i, h: (i, 0))],
        out_specs=pl.BlockSpec((None, None, tm, n), lambda i, h: (i // per_b, h, i % per_b, 0)),
        out_shape=jax.ShapeDtypeStruct((bsz, nh, seq, n), BF16),
        scratch_shapes=[pltpu.VMEM((tm, k), BF16)],
        compiler_params=_params(("parallel", "arbitrary")),
        name="qproj",
    )(proj, g, w_ext, cc, ss)


def _kvproj_kernel(kvc_ref, kr_ref, g_ref, w_ref, cc_ref, ss_ref, k_ref, v_ref, a_scr):
    @pl.when(pl.program_id(1) == 0)
    def _():
        a_scr[...] = _rms_bf16(kvc_ref[...], g_ref[...])

    acc = jnp.dot(a_scr[...], w_ref[...], preferred_element_type=F32)
    k_ref[:, 0:QK_NOPE] = acc[:, 0:QK_NOPE].astype(k_ref.dtype)
    k_ref[:, QK_NOPE:] = _rope_lanes(kr_ref[...], cc_ref[...], ss_ref[...]).astype(k_ref.dtype)
    v_ref[...] = acc[:, QK_NOPE:].astype(v_ref.dtype)


def _kvproj(proj, col_kvc, col_kr, g, w_h, cc, ss, bsz, seq):
    t = proj.shape[0]
    nh, k, n = w_h.shape
    tm = min(1024, seq)
    per_b = seq // tm
    omap = lambda i, h: (i // per_b, h, i % per_b, 0)
    return pl.pallas_call(
        _kvproj_kernel,
        grid=(t // tm, nh),
        in_specs=[pl.BlockSpec((tm, k), lambda i, h: (i, col_kvc // k)),
                  pl.BlockSpec((tm, LANES), lambda i, h: (i, col_kr // LANES)),
                  pl.BlockSpec((1, k), lambda i, h: (0, 0)),
                  pl.BlockSpec((None, k, n), lambda i, h: (h, 0, 0)),
                  pl.BlockSpec((tm, LANES), lambda i, h: (i, 0)),
                  pl.BlockSpec((tm, LANES), lambda i, h: (i, 0))],
        out_specs=[pl.BlockSpec((None, None, tm, 2 * LANES), omap),
                   pl.BlockSpec((None, None, tm, V_HEAD), omap)],
        out_shape=[jax.ShapeDtypeStruct((bsz, nh, seq, 2 * LANES), BF16),
                   jax.ShapeDtypeStruct((bsz, nh, seq, V_HEAD), BF16)],
        scratch_shapes=[pltpu.VMEM((tm, k), BF16)],
        compiler_params=_params(("parallel", "arbitrary")),
        name="kvproj",
    )(proj, proj, g, w_h, cc, ss)


def _attn_kernel(q_ref, k_ref, v_ref, o_ref, m_scr, l_scr, acc_scr, *, scale):
    qi, ki = pl.program_id(2), pl.program_id(3)
    tq, tk = q_ref.shape[0], k_ref.shape[0]

    @pl.when(ki == 0)
    def _():
        m_scr[...] = jnp.full_like(m_scr, NEG_INF)
        l_scr[...] = jnp.zeros_like(l_scr)
        acc_scr[...] = jnp.zeros_like(acc_scr)

    @pl.when(ki <= qi)
    def _():
        s = lax.dot_general(q_ref[...], k_ref[...], (((1,), (1,)), ((), ())),
                            preferred_element_type=F32) * scale
        row = qi * tq + lax.broadcasted_iota(jnp.int32, (tq, tk), 0)
        col = ki * tk + lax.broadcasted_iota(jnp.int32, (tq, tk), 1)
        s = jnp.where(col <= row, s, NEG_INF)
        m_prev = m_scr[...]
        m_new = jnp.maximum(m_prev, jnp.max(s, axis=-1, keepdims=True))
        alpha = jnp.exp(m_prev - m_new)
        p = jnp.exp(s - m_new)
        l_scr[...] = alpha * l_scr[...] + jnp.sum(p, axis=-1, keepdims=True)
        acc_scr[...] = alpha * acc_scr[...] + jnp.dot(p.astype(BF16), v_ref[...],
                                                      preferred_element_type=F32)
        m_scr[...] = m_new

    @pl.when(ki == qi)
    def _():
        o_ref[...] = (acc_scr[...] / l_scr[...]).astype(o_ref.dtype)


def _attention(q, k, v):
    bsz, nh, seq, dk = q.shape
    dv = v.shape[-1]
    tq = tk = min(1024, seq)
    scale = (QK_NOPE + QK_ROPE) ** -0.5
    return pl.pallas_call(
        functools.partial(_attn_kernel, scale=scale),
        grid=(bsz, nh, seq // tq, seq // tk),
        in_specs=[pl.BlockSpec((None, None, tq, dk), lambda b, h, i, j: (b, h, i, 0)),
                  pl.BlockSpec((None, None, tk, dk), lambda b, h, i, j: (b, h, jnp.minimum(i, j), 0)),
                  pl.BlockSpec((None, None, tk, dv), lambda b, h, i, j: (b, h, jnp.minimum(i, j), 0))],
        out_specs=pl.BlockSpec((None, tq, dv), lambda b, h, i, j: (b, i, h)),
        out_shape=jax.ShapeDtypeStruct((bsz, seq, nh * dv), BF16),
        scratch_shapes=[pltpu.VMEM((tq, 1), F32), pltpu.VMEM((tq, 1), F32), pltpu.VMEM((tq, dv), F32)],
        compiler_params=_params(("parallel", "parallel", "parallel", "arbitrary")),
        name="attn",
    )(q, k, v)


def _mix_kernel(hr_ref, o_ref, grnn_ref, gmla_ref, x_ref, gt_ref, w1_ref, w2_ref, wo_ref, lg_ref, lb_ref,
                out_ref, *, alpha):
    y_rnn = jnp.dot(hr_ref[...], w1_ref[...], preferred_element_type=F32)
    y_mla = jnp.dot(o_ref[...], w2_ref[...], preferred_element_type=F32)
    mixed = jax.nn.sigmoid(grnn_ref[...]) * y_rnn + jax.nn.sigmoid(gmla_ref[...]) * y_mla
    z = jnp.dot(mixed.astype(BF16), wo_ref[...], preferred_element_type=F32)
    z = alpha * x_ref[...] + (1.0 + gt_ref[0]) * z
    out_ref[...] = _layer_norm(z, lg_ref[...], lb_ref[...])


def _mix(hr, o, proj, col_grnn, col_gmla, x2, gt, w1, w2, wo, lg, lb, seq, alpha):
    t, d = x2.shape
    tm = min(256, seq)
    per_b = seq // tm
    rows = lambda width, cb=0: pl.BlockSpec((tm, width), lambda i: (i, cb))
    wspec = lambda w: _resident(w.shape, lambda i: (0, 0))
    return pl.pallas_call(
        functools.partial(_mix_kernel, alpha=alpha),
        grid=(t // tm,),
        in_specs=[rows(hr.shape[1]), rows(o.shape[1]),
                  rows(d, col_grnn // d), rows(d, col_gmla // d), rows(d),
                  pl.BlockSpec((1, 1, d), lambda i: (i // per_b, 0, 0)),
                  wspec(w1), wspec(w2), wspec(wo),
                  _resident((1, d), lambda i: (0, 0)), _resident((1, d), lambda i: (0, 0))],
        out_specs=rows(d),
        out_shape=jax.ShapeDtypeStruct((t, d), F32),
        compiler_params=_params(("parallel",)),
        name="mix",
    )(hr, o, proj, proj, x2, gt, w1, w2, wo, lg, lb)


def _topk_rows(s, k):
    n, tm = s.shape
    iota = lax.broadcasted_iota(jnp.int32, (n, tm), 0).astype(F32)
    krow = lax.broadcasted_iota(jnp.int32, (k, tm), 0)
    ts = jnp.zeros((k, tm), F32)
    ti = jnp.zeros((k, tm), F32)
    for kk in range(k):
        m = jnp.max(s, axis=0, keepdims=True)
        idx = jnp.min(jnp.where(s == m, iota, float(n)), axis=0, keepdims=True)
        ts = jnp.where(krow == kk, m, ts)
        ti = jnp.where(krow == kk, idx, ti)
        s = jnp.where(iota == idx, NEG_INF, s)
    return ts, ti


def _pair_topk(ts1, ti1, ts2, ti2, n_keys):
    k, tm = ts1.shape
    row8 = lax.broadcasted_iota(jnp.int32, (SUBLANES, tm), 0)
    rowk = lax.broadcasted_iota(jnp.int32, (k, tm), 0)
    cs, fs, es = [], [], []
    for i in range(k):
        nj = k // (i + 1)
        rows, jr = (k, rowk) if nj > SUBLANES else (SUBLANES, row8)
        c = ts1[i:i + 1, :] + ts2[0:rows, :]
        e = ti1[i:i + 1, :] * float(n_keys) + ti2[0:rows, :]
        cs.append(jnp.where(jr < nj, c, NEG_INF))
        fs.append((jr + i * k).astype(F32))
        es.append(e)
    cand = jnp.concatenate(cs, axis=0)
    flat = jnp.concatenate(fs, axis=0)
    expert = jnp.concatenate(es, axis=0)
    best_s = jnp.zeros((k, tm), F32)
    best_e = jnp.zeros((k, tm), F32)
    for kk in range(k):
        m = jnp.max(cand, axis=0, keepdims=True)
        fi = jnp.min(jnp.where(cand == m, flat, float(k * k)), axis=0, keepdims=True)
        sel = flat == fi
        ev = jnp.sum(jnp.where(sel, expert, 0.0), axis=0, keepdims=True)
        best_s = jnp.where(rowk == kk, m, best_s)
        best_e = jnp.where(rowk == kk, ev, best_e)
        cand = jnp.where(sel, NEG_INF, cand)
    return best_s, best_e


def _route_kernel(x_ref, sc_ref, sh_ref, wq_ref, keys_ref, h2_ref, e_ref, g_ref, q_scr, e_scr, g_scr,
                  *, n_heads):
    n_keys = keys_ref.shape[1]
    dk = keys_ref.shape[2]
    h2 = (x_ref[...] * (1.0 + sc_ref[0]) + sh_ref[0]).astype(BF16)
    h2_ref[...] = h2
    q = jnp.dot(h2, wq_ref[...], preferred_element_type=F32)
    for c in range(2 * n_heads):
        q_scr[c] = q[:, c * dk:(c + 1) * dk].astype(BF16)

    def head_body(h, carry):
        tops = []
        for p in range(2):
            s = lax.dot_general(keys_ref[p], q_scr[2 * h + p], (((1,), (1,)), ((), ())),
                                preferred_element_type=F32)
            tops.extend(_topk_rows(s, PEER_TOPK))
        best_s, best_e = _pair_topk(*tops, n_keys)
        ex = jnp.exp(best_s - jnp.max(best_s, axis=0, keepdims=True))
        gate = ex / jnp.sum(ex, axis=0, keepdims=True)
        r0 = pl.multiple_of(h * PEER_TOPK, PEER_TOPK)
        e_scr[pl.ds(r0, PEER_TOPK), :] = best_e
        g_scr[pl.ds(r0, PEER_TOPK), :] = gate
        return carry

    lax.fori_loop(0, n_heads, head_body, 0)
    e_ref[...] = e_scr[...].T
    g_ref[...] = g_scr[...].T


def _route(x1, sc, sh, wq, keys, seq):
    t, d = x1.shape
    nq = wq.shape[1]
    dk = keys.shape[2]
    n_heads = nq // (2 * dk)
    slots = n_heads * PEER_TOPK
    tm = min(256, seq)
    per_b = seq // tm
    return pl.pallas_call(
        functools.partial(_route_kernel, n_heads=n_heads),
        grid=(t // tm,),
        in_specs=[pl.BlockSpec((tm, d), lambda i: (i, 0)),
                  pl.BlockSpec((1, 1, d), lambda i: (i // per_b, 0, 0)),
                  pl.BlockSpec((1, 1, d), lambda i: (i // per_b, 0, 0)),
                  _resident(wq.shape, lambda i: (0, 0)),
                  _resident(keys.shape, lambda i: (0, 0, 0))],
        out_specs=[pl.BlockSpec((tm, d), lambda i: (i, 0)),
                   pl.BlockSpec((tm, slots), lambda i: (i, 0)),
                   pl.BlockSpec((tm, slots), lambda i: (i, 0))],
        out_shape=[jax.ShapeDtypeStruct((t, d), BF16),
                   jax.ShapeDtypeStruct((t, slots), F32),
                   jax.ShapeDtypeStruct((t, slots), F32)],
        scratch_shapes=[pltpu.VMEM((2 * n_heads, tm, dk), BF16),
                        pltpu.VMEM((slots, tm), F32),
                        pltpu.VMEM((slots, tm), F32)],
        compiler_params=_params(("parallel",)),
        name="route",
    )(x1, sc, sh, wq, keys)


def _gbuild_kernel(e_ref, g_ref, o_ref, *, n_keys):
    tmb, slots = e_ref.shape
    sub = lax.broadcasted_iota(jnp.int32, (n_keys, slots), 0).astype(F32)

    def body(t, carry):
        e = e_ref[pl.ds(t, 1), :]
        g = g_ref[pl.ds(t, 1), :]
        a = jnp.floor(e * (1.0 / n_keys))
        b = e - a * float(n_keys)
        p1 = jnp.where(sub == a, g, 0.0).astype(BF16)
        p2 = jnp.where(sub == b, 1.0, 0.0).astype(BF16)
        grid_t = lax.dot_general(p1, p2, (((1,), (1,)), ((), ())), preferred_element_type=F32)
        o_ref[t] = grid_t.astype(o_ref.dtype)
        return carry

    lax.fori_loop(0, tmb, body, 0)


def _gbuild(e, g, n_keys):
    t, slots = e.shape
    tmb = min(128, t)
    return pl.pallas_call(
        functools.partial(_gbuild_kernel, n_keys=n_keys),
        grid=(t // tmb,),
        in_specs=[pl.BlockSpec((tmb, slots), lambda i: (i, 0))] * 2,
        out_specs=pl.BlockSpec((tmb, n_keys, n_keys), lambda i: (i, 0, 0)),
        out_shape=jax.ShapeDtypeStruct((t, n_keys, n_keys), BF16),
        compiler_params=_params(("parallel",)),
        name="gbuild",
    )(e, g)


def _peer_kernel(h2_ref, ut_ref, v_ref, g_ref, x1_ref, gt_ref, lg_ref, lb_ref, o_ref, acc_scr, *, alpha):
    j = pl.program_id(1)

    @pl.when(j == 0)
    def _():
        acc_scr[...] = jnp.zeros_like(acc_scr)

    a = jnp.dot(h2_ref[...], ut_ref[...], preferred_element_type=F32)
    coef = (_gelu_tanh(a) * g_ref[...].astype(F32)).astype(BF16)
    acc_scr[...] += jnp.dot(coef, v_ref[...], preferred_element_type=F32)

    @pl.when(j == pl.num_programs(1) - 1)
    def _():
        z = alpha * x1_ref[...] + (1.0 + gt_ref[0]) * acc_scr[...]
        o_ref[...] = _layer_norm(z, lg_ref[...], lb_ref[...])


def _peer(h2, ut, v, g2, x1, gt, lg, lb, seq, alpha):
    t, d = h2.shape
    ne = v.shape[0]
    tm = min(512, seq)
    te = 512
    per_b = seq // tm
    return pl.pallas_call(
        functools.partial(_peer_kernel, alpha=alpha),
        grid=(t // tm, ne // te),
        in_specs=[pl.BlockSpec((tm, d), lambda i, j: (i, 0)),
                  pl.BlockSpec((d, te), lambda i, j: (0, j)),
                  pl.BlockSpec((te, d), lambda i, j: (j, 0)),
                  pl.BlockSpec((tm, te), lambda i, j: (i, j)),
                  pl.BlockSpec((tm, d), lambda i, j: (i, 0)),
                  pl.BlockSpec((1, 1, d), lambda i, j: (i // per_b, 0, 0)),
                  pl.BlockSpec((1, d), lambda i, j: (0, 0)),
                  pl.BlockSpec((1, d), lambda i, j: (0, 0))],
        out_specs=pl.BlockSpec((tm, d), lambda i, j: (i, 0)),
        out_shape=jax.ShapeDtypeStruct((t, d), F32),
        scratch_shapes=[pltpu.VMEM((tm, d), F32)],
        compiler_params=_params(("parallel", "arbitrary")),
        name="peer",
    )(h2, ut, v, g2, x1, gt, lg, lb)


def _swap_halves(w):
    half = w.shape[-1] // 2
    return jnp.concatenate([w[..., half:], w[..., :half]], axis=-1)


def _layer(x2, mod, cc, ss, bsz, seq, depth, p):
    t, d = x2.shape
    sh1, sc1, gt1, sh2, sc2, gt2 = [m[:, None, :] for m in jnp.split(mod, 6, axis=-1)]
    alpha = (2.0 * depth) ** 0.25
    d_rnn = p["conv_w"].shape[1]
    q_lora = p["q_norm_g"].shape[0]
    kv_lora = p["kv_norm_g"].shape[0]
    n_heads = p["w_uq"].shape[1] // (QK_NOPE + QK_ROPE)

    w_in = p["w_in"]
    widths = (d_rnn, d_rnn, q_lora, kv_lora, QK_ROPE, d, d)
    offs = [0]
    for wd in widths:
        offs.append(offs[-1] + wd)
    pieces = [w_in[:, offs[i]:offs[i + 1]] for i in range(7)]
    w_xr, w_gr, w_qc, w_kvc, w_kr, w_grnn, w_gmla = pieces
    kr_pad = 512
    w_kr_ext = jnp.concatenate([w_kr, _swap_halves(w_kr),
                                jnp.zeros((d, kr_pad - 2 * QK_ROPE), w_in.dtype)], axis=1)
    w_in_p = jnp.concatenate([w_xr, w_gr, w_grnn, w_gmla, w_qc, w_kvc, w_kr_ext], axis=1).astype(BF16)
    col_xr, col_gr, col_grnn, col_gmla = 0, d_rnn, 2 * d_rnn, 2 * d_rnn + d
    col_qc = 2 * d_rnn + 2 * d
    col_kvc = col_qc + q_lora
    col_kr = col_kvc + kv_lora

    proj = _proj(x2, sc1, sh1, w_in_p, seq)
    proj3 = proj.reshape(bsz, seq, proj.shape[1])

    row = lambda v: v[None, :]
    hr = _rglru(proj3, col_xr, col_gr, p["conv_w"], row(p["conv_b"]),
                p["w_rg_a"].astype(BF16), row(p["b_rg_a"]), p["w_rg_x"].astype(BF16), row(p["b_rg_x"]),
                row(p["rg_lambda"]))
    hr = hr.reshape(t, d_rnn)

    w_uq = p["w_uq"].reshape(q_lora, n_heads, QK_NOPE + QK_ROPE)
    w_rope = w_uq[..., QK_NOPE:]
    w_uq_ext = jnp.concatenate([w_uq[..., :QK_NOPE], w_rope, _swap_halves(w_rope)], axis=-1)
    w_uq_ext = w_uq_ext.transpose(1, 0, 2).astype(BF16)
    w_ukv = p["w_ukv"].reshape(kv_lora, n_heads, QK_NOPE + V_HEAD).transpose(1, 0, 2).astype(BF16)
    q = _qproj(proj, col_qc, row(p["q_norm_g"]), w_uq_ext, cc, ss, bsz, seq)
    k, v = _kvproj(proj, col_kvc, col_kr, row(p["kv_norm_g"]), w_ukv, cc, ss, bsz, seq)
    o = _attention(q, k, v)
    o = o.reshape(t, n_heads * V_HEAD)

    x1 = _mix(hr, o, proj, col_grnn, col_gmla, x2, gt1,
              p["w_rnn_out"].astype(BF16), p["w_mla_out"].astype(BF16), p["w_o"].astype(BF16),
              row(p["ln1_g"]), row(p["ln1_b"]), seq, alpha)

    keys = p["peer_keys"]
    n_keys = keys.shape[1]
    h2, e, g = _route(x1, sc2, sh2, p["peer_wq"].astype(BF16), keys.astype(BF16), seq)
    g3 = _gbuild(e, g, n_keys)
    g2 = g3.reshape(t, n_keys * n_keys)
    return _peer(h2, p["peer_u"].astype(BF16).T, p["peer_v"].astype(BF16), g2, x1, gt2,
                 row(p["ln2_g"]), row(p["ln2_b"]), seq, alpha)


def kernel(x, c, positions, w_ada, b_ada, w_in, conv_w, conv_b, w_rg_a, b_rg_a, w_rg_x, b_rg_x, rg_lambda,
           w_rnn_out, q_norm_g, w_uq, kv_norm_g, w_ukv, w_mla_out, w_o, ln1_g, ln1_b, peer_wq, peer_keys,
           peer_u, peer_v, ln2_g, ln2_b):
    bsz, seq, d = x.shape
    depth = w_ada.shape[0]
    t = bsz * seq
    stacked = dict(w_in=w_in, conv_w=conv_w, conv_b=conv_b, w_rg_a=w_rg_a, b_rg_a=b_rg_a, w_rg_x=w_rg_x,
                   b_rg_x=b_rg_x, rg_lambda=rg_lambda, w_rnn_out=w_rnn_out, q_norm_g=q_norm_g, w_uq=w_uq,
                   kv_norm_g=kv_norm_g, w_ukv=w_ukv, w_mla_out=w_mla_out, w_o=w_o, ln1_g=ln1_g, ln1_b=ln1_b,
                   peer_wq=peer_wq, peer_keys=peer_keys, peer_u=peer_u, peer_v=peer_v, ln2_g=ln2_g,
                   ln2_b=ln2_b)

    half = QK_ROPE // 2
    inv_freq = ROPE_THETA ** (-jnp.arange(0, QK_ROPE, 2, dtype=F32) / QK_ROPE)
    zeros = jnp.zeros((half,), F32)
    ones = jnp.ones((half,), F32)
    invf = jnp.concatenate([inv_freq, inv_freq, zeros, zeros])[None, :]
    cmask = jnp.concatenate([ones, ones, zeros, zeros])[None, :]
    smask = jnp.concatenate([-ones, ones, zeros, zeros])[None, :]
    cc, ss = _rope_tables(positions.reshape(t, 1), invf, cmask, smask)

    c_pad = jnp.concatenate([c, jnp.zeros((SUBLANES - bsz % SUBLANES, d), c.dtype)], axis=0)
    x2 = x.reshape(t, d)
    for l in range(depth):
        mod = _ada(c_pad, w_ada[l], b_ada[l][None, :])[:bsz]
        x2 = _layer(x2, mod, cc, ss, bsz, seq, depth, {k: v[l] for k, v in stacked.items()})
    return x2.reshape(bsz, seq, d)
```

```python
import functools
import math

import jax
import jax.numpy as jnp
from jax import lax
from jax.experimental import pallas as pl
from jax.experimental.pallas import tpu as pltpu

F32 = jnp.float32
BF16 = jnp.bfloat16

CONV_WIDTH = 4
LRU_C = 8.0
QK_NOPE = 128
QK_ROPE = 64
V_HEAD = 128
ROPE_THETA = 10000.0
PEER_TOPK = 16
LN_EPS = 1e-5
RMS_EPS = 1e-6
LANES = 128
SUBLANES = 8
VMEM_LIMIT = 56 * 1024 * 1024
NEG_INF = float("-inf")


def _params(sem):
    return pltpu.CompilerParams(dimension_semantics=sem, vmem_limit_bytes=VMEM_LIMIT)


def _resident(shape, index_map):
    return pl.BlockSpec(shape, index_map, pipeline_mode=pl.Buffered(1))


def _gelu_tanh(x):
    return 0.5 * x * (1.0 + jnp.tanh(math.sqrt(2.0 / math.pi) * (x + 0.044715 * (x * x * x))))


def _layer_norm(z, g, b):
    mu = jnp.mean(z, axis=-1, keepdims=True)
    d = z - mu
    var = jnp.mean(d * d, axis=-1, keepdims=True)
    return d * lax.rsqrt(var + LN_EPS) * g + b


def _ada_kernel(c_ref, w_ref, b_ref, o_ref):
    c = c_ref[...]
    ca = c * jax.nn.sigmoid(c)
    o_ref[...] = jnp.dot(ca, w_ref[...], preferred_element_type=F32,
                         precision=lax.Precision.HIGHEST) + b_ref[...]


def _ada(c_pad, w, b):
    m, d = c_pad.shape
    n = w.shape[1]
    tn = min(n, 1024)
    return pl.pallas_call(
        _ada_kernel,
        grid=(n // tn,),
        in_specs=[pl.BlockSpec((m, d), lambda j: (0, 0)),
                  pl.BlockSpec((d, tn), lambda j: (0, j)),
                  pl.BlockSpec((1, tn), lambda j: (0, j))],
        out_specs=pl.BlockSpec((m, tn), lambda j: (0, j)),
        out_shape=jax.ShapeDtypeStruct((m, n), F32),
        compiler_params=_params(("arbitrary",)),
        name="ada",
    )(c_pad, w, b)


def _proj_kernel(x_ref, sc_ref, sh_ref, w_ref, o_ref, a_scr):
    @pl.when(pl.program_id(1) == 0)
    def _():
        a_scr[...] = (x_ref[...] * (1.0 + sc_ref[0]) + sh_ref[0]).astype(BF16)

    o_ref[...] = jnp.dot(a_scr[...], w_ref[...], preferred_element_type=F32)


def _proj(x2, sc, sh, w, seq):
    t, d = x2.shape
    n = w.shape[1]
    tm = min(1024, seq)
    tn = 512
    per_b = seq // tm
    return pl.pallas_call(
        _proj_kernel,
        grid=(t // tm, n // tn),
        in_specs=[pl.BlockSpec((tm, d), lambda i, j: (i, 0)),
                  pl.BlockSpec((1, 1, d), lambda i, j: (i // per_b, 0, 0)),
                  pl.BlockSpec((1, 1, d), lambda i, j: (i // per_b, 0, 0)),
                  pl.BlockSpec((d, tn), lambda i, j: (0, j))],
        out_specs=pl.BlockSpec((tm, tn), lambda i, j: (i, j)),
        out_shape=jax.ShapeDtypeStruct((t, n), F32),
        scratch_shapes=[pltpu.VMEM((tm, d), BF16)],
        compiler_params=_params(("parallel", "arbitrary")),
        name="proj",
    )(x2, sc, sh, w)


def _rglru_kernel(xr_ref, gr_ref, cw_ref, cb_ref, wa_ref, ba_ref, wx_ref, bx_ref, lam_ref,
                  o_ref, xbuf, abuf, bbuf, h_scr):
    ts, tc = xr_ref.shape
    pad = SUBLANES

    @pl.when(pl.program_id(2) == 0)
    def _():
        xbuf[0:pad, :] = jnp.zeros((pad, tc), F32)
        h_scr[...] = jnp.zeros_like(h_scr)

    xbuf[pad:pad + ts, :] = xr_ref[...]
    xc = cb_ref[...] + cw_ref[0:1, :] * xbuf[pad - 3:pad - 3 + ts, :]
    for k in range(1, CONV_WIDTH):
        off = pad - (CONV_WIDTH - 1) + k
        xc = xc + cw_ref[k:k + 1, :] * xbuf[off:off + ts, :]
    xbuf[0:pad, :] = xbuf[ts:ts + pad, :]

    xcb = xc.astype(BF16)
    ra, ia = [], []
    for hh in range(tc // LANES):
        xh = xcb[:, hh * LANES:(hh + 1) * LANES]
        ra.append(jnp.dot(xh, wa_ref[hh], preferred_element_type=F32))
        ia.append(jnp.dot(xh, wx_ref[hh], preferred_element_type=F32))
    r = jax.nn.sigmoid(jnp.concatenate(ra, axis=1) + ba_ref[...])
    ig = jax.nn.sigmoid(jnp.concatenate(ia, axis=1) + bx_ref[...])
    nlam = -lam_ref[...]
    softplus = jnp.maximum(nlam, 0.0) + jnp.log1p(jnp.exp(-jnp.abs(nlam)))
    log_a = (-LRU_C * r) * softplus
    a = jnp.exp(log_a)
    b = jnp.sqrt(-jnp.tanh(log_a) * (a * a + 1.0)) * (ig * xc)

    abuf[0:ts, :] = jnp.ones((ts, tc), F32)
    bbuf[0:ts, :] = jnp.zeros((ts, tc), F32)
    d = 1
    while d < ts:
        abuf[ts:2 * ts, :] = a
        bbuf[ts:2 * ts, :] = b
        a_sh = abuf[ts - d:2 * ts - d, :]
        b_sh = bbuf[ts - d:2 * ts - d, :]
        b = a * b_sh + b
        a = a * a_sh
        d *= 2
    h = a * h_scr[...] + b
    h_scr[...] = h[ts - 1:ts, :]
    o_ref[...] = (h * _gelu_tanh(gr_ref[...])).astype(o_ref.dtype)


def _rglru(proj3, col_xr, col_gr, conv_w, conv_b, wa, ba, wx, bx, lam):
    bsz, seq, _ = proj3.shape
    d_rnn = conv_w.shape[1]
    tc = min(512, d_rnn)
    ts = min(512, seq)
    nc = d_rnn // tc
    hpt = tc // LANES
    vec = lambda: pl.BlockSpec((1, tc), lambda b, c, s: (0, c))
    return pl.pallas_call(
        _rglru_kernel,
        grid=(bsz, nc, seq // ts),
        in_specs=[pl.BlockSpec((None, ts, tc), lambda b, c, s: (b, s, col_xr // tc + c)),
                  pl.BlockSpec((None, ts, tc), lambda b, c, s: (b, s, col_gr // tc + c)),
                  pl.BlockSpec((CONV_WIDTH, tc), lambda b, c, s: (0, c)),
                  vec(),
                  pl.BlockSpec((hpt, LANES, LANES), lambda b, c, s: (c, 0, 0)),
                  vec(),
                  pl.BlockSpec((hpt, LANES, LANES), lambda b, c, s: (c, 0, 0)),
                  vec(), vec()],
        out_specs=pl.BlockSpec((None, ts, tc), lambda b, c, s: (b, s, c)),
        out_shape=jax.ShapeDtypeStruct((bsz, seq, d_rnn), BF16),
        scratch_shapes=[pltpu.VMEM((ts + 2 * SUBLANES, tc), F32),
                        pltpu.VMEM((2 * ts, tc), F32),
                        pltpu.VMEM((2 * ts, tc), F32),
                        pltpu.VMEM((1, tc), F32)],
        compiler_params=_params(("parallel", "parallel", "arbitrary")),
        name="rglru",
    )(proj3, proj3, conv_w, conv_b, wa, ba, wx, bx, lam)


def _rope_kernel(pos_ref, invf_ref, cm_ref, sm_ref, cc_ref, ss_ref):
    ang = pos_ref[...].astype(F32) * invf_ref[...]
    cc_ref[...] = jnp.cos(ang) * cm_ref[...]
    ss_ref[...] = jnp.sin(ang) * sm_ref[...]


def _rope_tables(pos_col, invf, cmask, smask):
    t = pos_col.shape[0]
    tm = min(1024, t)
    row = lambda: pl.BlockSpec((1, LANES), lambda i: (0, 0))
    return pl.pallas_call(
        _rope_kernel,
        grid=(t // tm,),
        in_specs=[pl.BlockSpec((tm, 1), lambda i: (i, 0)), row(), row(), row()],
        out_specs=[pl.BlockSpec((tm, LANES), lambda i: (i, 0))] * 2,
        out_shape=[jax.ShapeDtypeStruct((t, LANES), F32)] * 2,
        compiler_params=_params(("parallel",)),
        name="rope",
    )(pos_col, invf, cmask, smask)


def _rms_bf16(x, g):
    return (x * lax.rsqrt(jnp.mean(x * x, axis=-1, keepdims=True) + RMS_EPS) * g).astype(BF16)


def _rope_lanes(r, cc, ss):
    return r * cc + pltpu.roll(r, LANES // 2, axis=1) * ss


def _qproj_kernel(qc_ref, g_ref, w_ref, cc_ref, ss_ref, o_ref, a_scr):
    @pl.when(pl.program_id(1) == 0)
    def _():
        a_scr[...] = _rms_bf16(qc_ref[...], g_ref[...])

    acc = jnp.dot(a_scr[...], w_ref[...], preferred_element_type=F32)
    qscale = (QK_NOPE + QK_ROPE) ** -0.5 * math.log2(math.e)
    o_ref[:, 0:QK_NOPE] = (acc[:, 0:QK_NOPE] * qscale).astype(o_ref.dtype)
    roped = _rope_lanes(acc[:, QK_NOPE:], cc_ref[...], ss_ref[...])
    o_ref[:, QK_NOPE:] = (roped * qscale).astype(o_ref.dtype)


def _qproj(proj, col_qc, g, w_ext, cc, ss, bsz, seq):
    t = proj.shape[0]
    nh, k, n = w_ext.shape
    tm = min(1024, seq)
    per_b = seq // tm
    return pl.pallas_call(
        _qproj_kernel,
        grid=(t // tm, nh),
        in_specs=[pl.BlockSpec((tm, k), lambda i, h: (i, col_qc // k)),
                  pl.BlockSpec((1, k), lambda i, h: (0, 0)),
                  pl.BlockSpec((None, k, n), lambda i, h: (h, 0, 0)),
                  pl.BlockSpec((tm, LANES), lambda i, h: (i, 0)),
                  pl.BlockSpec((tm, LANES), lambda i, h: (i, 0))],
        out_specs=pl.BlockSpec((None, None, tm, n), lambda i, h: (i // per_b, h, i % per_b, 0)),
        out_shape=jax.ShapeDtypeStruct((bsz, nh, seq, n), BF16),
        scratch_shapes=[pltpu.VMEM((tm, k), BF16)],
        compiler_params=_params(("parallel", "arbitrary")),
        name="qproj",
    )(proj, g, w_ext, cc, ss)


def _kvproj_kernel(kvc_ref, kr_ref, g_ref, w_ref, cc_ref, ss_ref, k_ref, v_ref, a_scr):
    @pl.when(pl.program_id(1) == 0)
    def _():
        a_scr[...] = _rms_bf16(kvc_ref[...], g_ref[...])

    acc = jnp.dot(a_scr[...], w_ref[...], preferred_element_type=F32)
    k_ref[:, 0:QK_NOPE] = acc[:, 0:QK_NOPE].astype(k_ref.dtype)
    k_ref[:, QK_NOPE:] = _rope_lanes(kr_ref[...], cc_ref[...], ss_ref[...]).astype(k_ref.dtype)
    v_ref[...] = acc[:, QK_NOPE:].astype(v_ref.dtype)


def _kvproj(proj, col_kvc, col_kr, g, w_h, cc, ss, bsz, seq):
    t = proj.shape[0]
    nh, k, n = w_h.shape
    tm = min(1024, seq)
    per_b = seq // tm
    omap = lambda i, h: (i // per_b, h, i % per_b, 0)
    return pl.pallas_call(
        _kvproj_kernel,
        grid=(t // tm, nh),
        in_specs=[pl.BlockSpec((tm, k), lambda i, h: (i, col_kvc // k)),
                  pl.BlockSpec((tm, LANES), lambda i, h: (i, col_kr // LANES)),
                  pl.BlockSpec((1, k), lambda i, h: (0, 0)),
                  pl.BlockSpec((None, k, n), lambda i, h: (h, 0, 0)),
                  pl.BlockSpec((tm, LANES), lambda i, h: (i, 0)),
                  pl.BlockSpec((tm, LANES), lambda i, h: (i, 0))],
        out_specs=[pl.BlockSpec((None, None, tm, 2 * LANES), omap),
                   pl.BlockSpec((None, None, tm, V_HEAD), omap)],
        out_shape=[jax.ShapeDtypeStruct((bsz, nh, seq, 2 * LANES), BF16),
                   jax.ShapeDtypeStruct((bsz, nh, seq, V_HEAD), BF16)],
        scratch_shapes=[pltpu.VMEM((tm, k), BF16)],
        compiler_params=_params(("parallel", "arbitrary")),
        name="kvproj",
    )(proj, proj, g, w_h, cc, ss)


ATTN_ROWS = 32


def _attn_kernel(qi_tab, ki_tab, q_ref, k_ref, v_ref, o_ref, m_scr, a_scr, acc_scr, s_scr, p_scr):
    step = pl.program_id(2)
    qi, ki = qi_tab[step], ki_tab[step]
    tq, tk = q_ref.shape[0], k_ref.shape[0]
    dv = v_ref.shape[1]

    @pl.when(ki == 0)
    def _():
        m_scr[...] = jnp.full_like(m_scr, NEG_INF)
        acc_scr[...] = jnp.zeros_like(acc_scr)

    def block(masked):
        half = tq // 2
        for hf in range(2):
            hrows = slice(hf * half, (hf + 1) * half)
            s_scr[hrows, :] = lax.dot_general(q_ref[hrows, :], k_ref[...], (((1,), (1,)), ((), ())),
                                              preferred_element_type=F32)
        n_lane = tk // LANES
        for c in range(tq // ATTN_ROWS):
            rows = slice(c * ATTN_ROWS, (c + 1) * ATTN_ROWS)
            tmax = None
            for j in range(n_lane):
                cols = slice(j * LANES, (j + 1) * LANES)
                s = s_scr[rows, cols]
                if masked:
                    if j * LANES > (c + 1) * ATTN_ROWS - 1:
                        continue
                    if (j + 1) * LANES - 1 > c * ATTN_ROWS:
                        row = c * ATTN_ROWS + lax.broadcasted_iota(jnp.int32, (ATTN_ROWS, LANES), 0)
                        col = j * LANES + lax.broadcasted_iota(jnp.int32, (ATTN_ROWS, LANES), 1)
                        s = jnp.where(col <= row, s, NEG_INF)
                        s_scr[rows, cols] = s
                tmax = s if tmax is None else jnp.maximum(tmax, s)
            m_prev = m_scr[rows, :]
            m_new = jnp.maximum(m_prev, jnp.max(tmax, axis=-1, keepdims=True))
            a_scr[rows, :] = jnp.exp2(m_prev - m_new)
            m_scr[rows, :] = m_new
        for c in range(tq // ATTN_ROWS):
            rows = slice(c * ATTN_ROWS, (c + 1) * ATTN_ROWS)
            m_new = m_scr[rows, :]
            for j in range(n_lane):
                cols = slice(j * LANES, (j + 1) * LANES)
                if masked and j * LANES > (c + 1) * ATTN_ROWS - 1:
                    p_scr[rows, cols] = jnp.zeros((ATTN_ROWS, LANES), BF16)
                else:
                    p_scr[rows, cols] = jnp.exp2(s_scr[rows, cols] - m_new).astype(BF16)
        v_ones = jnp.concatenate([v_ref[...], jnp.ones((tk, dv), BF16)], axis=1)
        for hf in range(2):
            hrows = slice(hf * half, (hf + 1) * half)
            pv = jnp.dot(p_scr[hrows, :], v_ones, preferred_element_type=F32)
            alpha = a_scr[hrows, :]
            acc_scr[hrows, :] = jnp.concatenate([alpha] * (2 * dv // LANES), axis=1) * acc_scr[hrows, :] + pv

    pl.when(ki < qi)(functools.partial(block, False))

    @pl.when(ki == qi)
    def _():
        block(True)
        acc = acc_scr[...]
        o_ref[...] = (acc[:, :dv] / acc[:, dv:]).astype(o_ref.dtype)


def _attention(q, k, v):
    bsz, nh, seq, dk = q.shape
    dv = v.shape[-1]
    tq = tk = min(1024, seq)
    nq = seq // tq
    pairs = [(i, j) for i in range(nq) for j in range(i + 1)]
    qi_tab = jnp.asarray([p[0] for p in pairs], jnp.int32)
    ki_tab = jnp.asarray([p[1] for p in pairs], jnp.int32)
    grid_spec = pltpu.PrefetchScalarGridSpec(
        num_scalar_prefetch=2,
        grid=(bsz, nh, len(pairs)),
        in_specs=[pl.BlockSpec((None, None, tq, dk), lambda b, h, s, qt, kt: (b, h, qt[s], 0)),
                  pl.BlockSpec((None, None, tk, dk), lambda b, h, s, qt, kt: (b, h, kt[s], 0)),
                  pl.BlockSpec((None, None, tk, dv), lambda b, h, s, qt, kt: (b, h, kt[s], 0))],
        out_specs=pl.BlockSpec((None, tq, dv), lambda b, h, s, qt, kt: (b, qt[s], h)),
        scratch_shapes=[pltpu.VMEM((tq, LANES), F32), pltpu.VMEM((tq, LANES), F32),
                        pltpu.VMEM((tq, 2 * dv), F32), pltpu.VMEM((tq, tk), F32), pltpu.VMEM((tq, tk), BF16)],
    )
    return pl.pallas_call(
        _attn_kernel,
        grid_spec=grid_spec,
        out_shape=jax.ShapeDtypeStruct((bsz, seq, nh * dv), BF16),
        compiler_params=_params(("parallel", "parallel", "arbitrary")),
        name="attn",
    )(qi_tab, ki_tab, q, k, v)


def _mix_kernel(hr_ref, o_ref, grnn_ref, gmla_ref, x_ref, gt_ref, w1_ref, w2_ref, wo_ref, lg_ref, lb_ref,
                out_ref, *, alpha):
    y_rnn = jnp.dot(hr_ref[...], w1_ref[...], preferred_element_type=F32)
    y_mla = jnp.dot(o_ref[...], w2_ref[...], preferred_element_type=F32)
    mixed = jax.nn.sigmoid(grnn_ref[...]) * y_rnn + jax.nn.sigmoid(gmla_ref[...]) * y_mla
    z = jnp.dot(mixed.astype(BF16), wo_ref[...], preferred_element_type=F32)
    z = alpha * x_ref[...] + (1.0 + gt_ref[0]) * z
    out_ref[...] = _layer_norm(z, lg_ref[...], lb_ref[...])


def _mix(hr, o, proj, col_grnn, col_gmla, x2, gt, w1, w2, wo, lg, lb, seq, alpha):
    t, d = x2.shape
    tm = min(256, seq)
    per_b = seq // tm
    rows = lambda width, cb=0: pl.BlockSpec((tm, width), lambda i: (i, cb))
    wspec = lambda w: _resident(w.shape, lambda i: (0, 0))
    return pl.pallas_call(
        functools.partial(_mix_kernel, alpha=alpha),
        grid=(t // tm,),
        in_specs=[rows(hr.shape[1]), rows(o.shape[1]),
                  rows(d, col_grnn // d), rows(d, col_gmla // d), rows(d),
                  pl.BlockSpec((1, 1, d), lambda i: (i // per_b, 0, 0)),
                  wspec(w1), wspec(w2), wspec(wo),
                  _resident((1, d), lambda i: (0, 0)), _resident((1, d), lambda i: (0, 0))],
        out_specs=rows(d),
        out_shape=jax.ShapeDtypeStruct((t, d), F32),
        compiler_params=_params(("parallel",)),
        name="mix",
    )(hr, o, proj, proj, x2, gt, w1, w2, wo, lg, lb)


def _topk_rows(s, k):
    n, tm = s.shape
    iota = lax.broadcasted_iota(jnp.int32, (n, tm), 0).astype(F32)
    krow = lax.broadcasted_iota(jnp.int32, (k, tm), 0)
    ts = jnp.zeros((k, tm), F32)
    ti = jnp.zeros((k, tm), F32)
    for kk in range(k):
        m = jnp.max(s, axis=0, keepdims=True)
        idx = jnp.min(jnp.where(s == m, iota, float(n)), axis=0, keepdims=True)
        ts = jnp.where(krow == kk, m, ts)
        ti = jnp.where(krow == kk, idx, ti)
        s = jnp.where(iota == idx, NEG_INF, s)
    return ts, ti


def _pair_topk(ts1, ti1, ts2, ti2, n_keys):
    k, tm = ts1.shape
    row8 = lax.broadcasted_iota(jnp.int32, (SUBLANES, tm), 0)
    rowk = lax.broadcasted_iota(jnp.int32, (k, tm), 0)
    cs, fs, es = [], [], []
    for i in range(k):
        nj = k // (i + 1)
        rows, jr = (k, rowk) if nj > SUBLANES else (SUBLANES, row8)
        c = ts1[i:i + 1, :] + ts2[0:rows, :]
        e = ti1[i:i + 1, :] * float(n_keys) + ti2[0:rows, :]
        cs.append(jnp.where(jr < nj, c, NEG_INF))
        fs.append((jr + i * k).astype(F32))
        es.append(e)
    cand = jnp.concatenate(cs, axis=0)
    flat = jnp.concatenate(fs, axis=0)
    expert = jnp.concatenate(es, axis=0)
    best_s = jnp.zeros((k, tm), F32)
    best_e = jnp.zeros((k, tm), F32)
    for kk in range(k):
        m = jnp.max(cand, axis=0, keepdims=True)
        fi = jnp.min(jnp.where(cand == m, flat, float(k * k)), axis=0, keepdims=True)
        sel = flat == fi
        ev = jnp.sum(jnp.where(sel, expert, 0.0), axis=0, keepdims=True)
        best_s = jnp.where(rowk == kk, m, best_s)
        best_e = jnp.where(rowk == kk, ev, best_e)
        cand = jnp.where(sel, NEG_INF, cand)
    return best_s, best_e


def _route_kernel(x_ref, sc_ref, sh_ref, wq_ref, keys_ref, h2_ref, e_ref, g_ref, q_scr, e_scr, g_scr,
                  *, n_heads):
    n_keys = keys_ref.shape[1]
    dk = keys_ref.shape[2]
    h2 = (x_ref[...] * (1.0 + sc_ref[0]) + sh_ref[0]).astype(BF16)
    h2_ref[...] = h2
    q = jnp.dot(h2, wq_ref[...], preferred_element_type=F32)
    for c in range(2 * n_heads):
        q_scr[c] = q[:, c * dk:(c + 1) * dk].astype(BF16)

    def head_body(h, carry):
        tops = []
        for p in range(2):
            s = lax.dot_general(keys_ref[p], q_scr[2 * h + p], (((1,), (1,)), ((), ())),
                                preferred_element_type=F32)
            tops.extend(_topk_rows(s, PEER_TOPK))
        best_s, best_e = _pair_topk(*tops, n_keys)
        ex = jnp.exp(best_s - jnp.max(best_s, axis=0, keepdims=True))
        gate = ex / jnp.sum(ex, axis=0, keepdims=True)
        r0 = pl.multiple_of(h * PEER_TOPK, PEER_TOPK)
        e_scr[pl.ds(r0, PEER_TOPK), :] = best_e
        g_scr[pl.ds(r0, PEER_TOPK), :] = gate
        return carry

    lax.fori_loop(0, n_heads, head_body, 0)
    e_ref[...] = e_scr[...].T
    g_ref[...] = g_scr[...].T


def _route(x1, sc, sh, wq, keys, seq):
    t, d = x1.shape
    nq = wq.shape[1]
    dk = keys.shape[2]
    n_heads = nq // (2 * dk)
    slots = n_heads * PEER_TOPK
    tm = min(256, seq)
    per_b = seq // tm
    return pl.pallas_call(
        functools.partial(_route_kernel, n_heads=n_heads),
        grid=(t // tm,),
        in_specs=[pl.BlockSpec((tm, d), lambda i: (i, 0)),
                  pl.BlockSpec((1, 1, d), lambda i: (i // per_b, 0, 0)),
                  pl.BlockSpec((1, 1, d), lambda i: (i // per_b, 0, 0)),
                  _resident(wq.shape, lambda i: (0, 0)),
                  _resident(keys.shape, lambda i: (0, 0, 0))],
        out_specs=[pl.BlockSpec((tm, d), lambda i: (i, 0)),
                   pl.BlockSpec((tm, slots), lambda i: (i, 0)),
                   pl.BlockSpec((tm, slots), lambda i: (i, 0))],
        out_shape=[jax.ShapeDtypeStruct((t, d), BF16),
                   jax.ShapeDtypeStruct((t, slots), F32),
                   jax.ShapeDtypeStruct((t, slots), F32)],
        scratch_shapes=[pltpu.VMEM((2 * n_heads, tm, dk), BF16),
                        pltpu.VMEM((slots, tm), F32),
                        pltpu.VMEM((slots, tm), F32)],
        compiler_params=_params(("parallel",)),
        name="route",
    )(x1, sc, sh, wq, keys)


GB_TOKENS = 128
GB_PITCH = GB_TOKENS + SUBLANES
GB_UNROLL = 8


def _gbuild_kernel(e_ref, g_ref, o_ref, *, n_keys):
    tmb, slots = e_ref.shape
    sub = lax.broadcasted_iota(jnp.int32, (n_keys, slots), 0).astype(F32)
    zeros = jnp.zeros((n_keys, n_keys), F32)
    for r in range(tmb, GB_PITCH):
        o_ref[pl.ds(r, n_keys, stride=GB_PITCH), :] = zeros

    def body(tb, carry):
        for u in range(GB_UNROLL):
            t = tb * GB_UNROLL + u
            e = e_ref[pl.ds(t, 1), :]
            g = g_ref[pl.ds(t, 1), :]
            a = jnp.floor(e * (1.0 / n_keys))
            b = e - a * float(n_keys)
            p1 = jnp.where(sub == a, g, 0.0).astype(BF16)
            p2 = jnp.where(sub == b, 1.0, 0.0).astype(BF16)
            grid_t = lax.dot_general(p1, p2, (((1,), (1,)), ((), ())), preferred_element_type=F32)
            o_ref[pl.ds(t, n_keys, stride=GB_PITCH), :] = grid_t
        return carry

    lax.fori_loop(0, tmb // GB_UNROLL, body, 0)


def _gbuild(e, g, n_keys):
    t, slots = e.shape
    tmb = GB_TOKENS
    return pl.pallas_call(
        functools.partial(_gbuild_kernel, n_keys=n_keys),
        grid=(t // tmb,),
        in_specs=[pl.BlockSpec((tmb, slots), lambda i: (i, 0))] * 2,
        out_specs=pl.BlockSpec((None, n_keys * GB_PITCH, n_keys), lambda i: (i, 0, 0)),
        out_shape=jax.ShapeDtypeStruct((t // tmb, n_keys * GB_PITCH, n_keys), F32),
        compiler_params=_params(("parallel",)),
        name="gbuild",
    )(e, g)


def _peer_kernel(h2_ref, ut_ref, v_ref, g_ref, x1_ref, gt_ref, lg_ref, lb_ref, o_ref, acc_scr, *, alpha):
    j = pl.program_id(1)

    @pl.when(j == 0)
    def _():
        acc_scr[...] = jnp.zeros_like(acc_scr)

    a = jnp.dot(h2_ref[...], ut_ref[...], preferred_element_type=F32)
    nblk, n1, _, n_keys = g_ref.shape
    tm = a.shape[0]
    coef = []
    for k in range(n1):
        gate = g_ref[:, k, 0:GB_TOKENS, :].reshape(tm, n_keys)
        coef.append((_gelu_tanh(a[:, k * n_keys:(k + 1) * n_keys]) * gate).astype(BF16))
    acc_scr[...] += jnp.dot(jnp.concatenate(coef, axis=1), v_ref[...], preferred_element_type=F32)

    @pl.when(j == pl.num_programs(1) - 1)
    def _():
        z = alpha * x1_ref[...] + (1.0 + gt_ref[0]) * acc_scr[...]
        o_ref[...] = _layer_norm(z, lg_ref[...], lb_ref[...])


def _peer(h2, ut, v, g4, x1, gt, lg, lb, seq, alpha):
    t, d = h2.shape
    ne = v.shape[0]
    n_keys = g4.shape[3]
    tm = min(512, seq)
    te = 512
    n1 = te // n_keys
    per_b = seq // tm
    return pl.pallas_call(
        functools.partial(_peer_kernel, alpha=alpha),
        grid=(t // tm, ne // te),
        in_specs=[pl.BlockSpec((tm, d), lambda i, j: (i, 0)),
                  pl.BlockSpec((d, te), lambda i, j: (0, j)),
                  pl.BlockSpec((te, d), lambda i, j: (j, 0)),
                  pl.BlockSpec((tm // GB_TOKENS, n1, GB_PITCH, n_keys), lambda i, j: (i, j, 0, 0)),
                  pl.BlockSpec((tm, d), lambda i, j: (i, 0)),
                  pl.BlockSpec((1, 1, d), lambda i, j: (i // per_b, 0, 0)),
                  pl.BlockSpec((1, d), lambda i, j: (0, 0)),
                  pl.BlockSpec((1, d), lambda i, j: (0, 0))],
        out_specs=pl.BlockSpec((tm, d), lambda i, j: (i, 0)),
        out_shape=jax.ShapeDtypeStruct((t, d), F32),
        scratch_shapes=[pltpu.VMEM((tm, d), F32)],
        compiler_params=_params(("parallel", "arbitrary")),
        name="peer",
    )(h2, ut, v, g4, x1, gt, lg, lb)


def _swap_halves(w):
    half = w.shape[-1] // 2
    return jnp.concatenate([w[..., half:], w[..., :half]], axis=-1)


def _layer(x2, mod, cc, ss, bsz, seq, depth, p):
    t, d = x2.shape
    sh1, sc1, gt1, sh2, sc2, gt2 = [m[:, None, :] for m in jnp.split(mod, 6, axis=-1)]
    alpha = (2.0 * depth) ** 0.25
    d_rnn = p["conv_w"].shape[1]
    q_lora = p["q_norm_g"].shape[0]
    kv_lora = p["kv_norm_g"].shape[0]
    n_heads = p["w_uq"].shape[1] // (QK_NOPE + QK_ROPE)

    w_in = p["w_in"]
    widths = (d_rnn, d_rnn, q_lora, kv_lora, QK_ROPE, d, d)
    offs = [0]
    for wd in widths:
        offs.append(offs[-1] + wd)
    pieces = [w_in[:, offs[i]:offs[i + 1]] for i in range(7)]
    w_xr, w_gr, w_qc, w_kvc, w_kr, w_grnn, w_gmla = pieces
    kr_pad = 512
    w_kr_ext = jnp.concatenate([w_kr, _swap_halves(w_kr),
                                jnp.zeros((d, kr_pad - 2 * QK_ROPE), w_in.dtype)], axis=1)
    w_in_p = jnp.concatenate([w_xr, w_gr, w_grnn, w_gmla, w_qc, w_kvc, w_kr_ext], axis=1).astype(BF16)
    col_xr, col_gr, col_grnn, col_gmla = 0, d_rnn, 2 * d_rnn, 2 * d_rnn + d
    col_qc = 2 * d_rnn + 2 * d
    col_kvc = col_qc + q_lora
    col_kr = col_kvc + kv_lora

    proj = _proj(x2, sc1, sh1, w_in_p, seq)
    proj3 = proj.reshape(bsz, seq, proj.shape[1])

    row = lambda v: v[None, :]
    hr = _rglru(proj3, col_xr, col_gr, p["conv_w"], row(p["conv_b"]),
                p["w_rg_a"].astype(BF16), row(p["b_rg_a"]), p["w_rg_x"].astype(BF16), row(p["b_rg_x"]),
                row(p["rg_lambda"]))
    hr = hr.reshape(t, d_rnn)

    w_uq = p["w_uq"].reshape(q_lora, n_heads, QK_NOPE + QK_ROPE)
    w_rope = w_uq[..., QK_NOPE:]
    w_uq_ext = jnp.concatenate([w_uq[..., :QK_NOPE], w_rope, _swap_halves(w_rope)], axis=-1)
    w_uq_ext = w_uq_ext.transpose(1, 0, 2).astype(BF16)
    w_ukv = p["w_ukv"].reshape(kv_lora, n_heads, QK_NOPE + V_HEAD).transpose(1, 0, 2).astype(BF16)
    q = _qproj(proj, col_qc, row(p["q_norm_g"]), w_uq_ext, cc, ss, bsz, seq)
    k, v = _kvproj(proj, col_kvc, col_kr, row(p["kv_norm_g"]), w_ukv, cc, ss, bsz, seq)
    o = _attention(q, k, v)
    o = o.reshape(t, n_heads * V_HEAD)

    x1 = _mix(hr, o, proj, col_grnn, col_gmla, x2, gt1,
              p["w_rnn_out"].astype(BF16), p["w_mla_out"].astype(BF16), p["w_o"].astype(BF16),
              row(p["ln1_g"]), row(p["ln1_b"]), seq, alpha)

    keys = p["peer_keys"]
    n_keys = keys.shape[1]
    h2, e, g = _route(x1, sc2, sh2, p["peer_wq"].astype(BF16), keys.astype(BF16), seq)
    g3 = _gbuild(e, g, n_keys)
    g4 = g3.reshape(g3.shape[0], n_keys, GB_PITCH, n_keys)
    return _peer(h2, p["peer_u"].astype(BF16).T, p["peer_v"].astype(BF16), g4, x1, gt2,
                 row(p["ln2_g"]), row(p["ln2_b"]), seq, alpha)


def kernel(x, c, positions, w_ada, b_ada, w_in, conv_w, conv_b, w_rg_a, b_rg_a, w_rg_x, b_rg_x, rg_lambda,
           w_rnn_out, q_norm_g, w_uq, kv_norm_g, w_ukv, w_mla_out, w_o, ln1_g, ln1_b, peer_wq, peer_keys,
           peer_u, peer_v, ln2_g, ln2_b):
    bsz, seq, d = x.shape
    depth = w_ada.shape[0]
    t = bsz * seq
    stacked = dict(w_in=w_in, conv_w=conv_w, conv_b=conv_b, w_rg_a=w_rg_a, b_rg_a=b_rg_a, w_rg_x=w_rg_x,
                   b_rg_x=b_rg_x, rg_lambda=rg_lambda, w_rnn_out=w_rnn_out, q_norm_g=q_norm_g, w_uq=w_uq,
                   kv_norm_g=kv_norm_g, w_ukv=w_ukv, w_mla_out=w_mla_out, w_o=w_o, ln1_g=ln1_g, ln1_b=ln1_b,
                   peer_wq=peer_wq, peer_keys=peer_keys, peer_u=peer_u, peer_v=peer_v, ln2_g=ln2_g,
                   ln2_b=ln2_b)

    half = QK_ROPE // 2
    inv_freq = ROPE_THETA ** (-jnp.arange(0, QK_ROPE, 2, dtype=F32) / QK_ROPE)
    zeros = jnp.zeros((half,), F32)
    ones = jnp.ones((half,), F32)
    invf = jnp.concatenate([inv_freq, inv_freq, zeros, zeros])[None, :]
    cmask = jnp.concatenate([ones, ones, zeros, zeros])[None, :]
    smask = jnp.concatenate([-ones, ones, zeros, zeros])[None, :]
    cc, ss = _rope_tables(positions.reshape(t, 1), invf, cmask, smask)

    c_pad = jnp.concatenate([c, jnp.zeros((SUBLANES - bsz % SUBLANES, d), c.dtype)], axis=0)
    x2 = x.reshape(t, d)
    for l in range(depth):
        mod = _ada(c_pad, w_ada[l], b_ada[l][None, :])[:bsz]
        x2 = _layer(x2, mod, cc, ss, bsz, seq, depth, {k: v[l] for k, v in stacked.items()})
    return x2.reshape(bsz, seq, d)
```

```python
import functools
import math

import jax
import jax.numpy as jnp
from jax import lax
from jax.experimental import pallas as pl
from jax.experimental.pallas import tpu as pltpu

F32 = jnp.float32
BF16 = jnp.bfloat16

CONV_WIDTH = 4
LRU_C = 8.0
QK_NOPE = 128
QK_ROPE = 64
V_HEAD = 128
ROPE_THETA = 10000.0
PEER_TOPK = 16
LN_EPS = 1e-5
RMS_EPS = 1e-6
LANES = 128
SUBLANES = 8
VMEM_LIMIT = 56 * 1024 * 1024
NEG_INF = float("-inf")


def _params(sem):
    return pltpu.CompilerParams(dimension_semantics=sem, vmem_limit_bytes=VMEM_LIMIT)


def _resident(shape, index_map):
    return pl.BlockSpec(shape, index_map, pipeline_mode=pl.Buffered(1))


def _gelu_tanh(x):
    return 0.5 * x * (1.0 + jnp.tanh(math.sqrt(2.0 / math.pi) * (x + 0.044715 * (x * x * x))))


def _layer_norm(z, g, b):
    mu = jnp.mean(z, axis=-1, keepdims=True)
    d = z - mu
    var = jnp.mean(d * d, axis=-1, keepdims=True)
    return d * lax.rsqrt(var + LN_EPS) * g + b


def _ada_kernel(c_ref, w_ref, b_ref, o_ref):
    c = c_ref[...]
    ca = c * jax.nn.sigmoid(c)
    o_ref[...] = jnp.dot(ca, w_ref[...], preferred_element_type=F32,
                         precision=lax.Precision.HIGHEST) + b_ref[...]


def _ada(c_pad, w, b):
    m, d = c_pad.shape
    n = w.shape[1]
    tn = min(n, 1024)
    return pl.pallas_call(
        _ada_kernel,
        grid=(n // tn,),
        in_specs=[pl.BlockSpec((m, d), lambda j: (0, 0)),
                  pl.BlockSpec((d, tn), lambda j: (0, j)),
                  pl.BlockSpec((1, tn), lambda j: (0, j))],
        out_specs=pl.BlockSpec((m, tn), lambda j: (0, j)),
        out_shape=jax.ShapeDtypeStruct((m, n), F32),
        compiler_params=_params(("arbitrary",)),
        name="ada",
    )(c_pad, w, b)


def _proj_kernel(x_ref, sc_ref, sh_ref, w_ref, o_ref, a_scr):
    @pl.when(pl.program_id(1) == 0)
    def _():
        a_scr[...] = (x_ref[...] * (1.0 + sc_ref[0]) + sh_ref[0]).astype(BF16)

    o_ref[...] = jnp.dot(a_scr[...], w_ref[...], preferred_element_type=F32)


def _proj(x2, sc, sh, w, seq):
    t, d = x2.shape
    n = w.shape[1]
    tm = min(1024, seq)
    tn = 512
    per_b = seq // tm
    return pl.pallas_call(
        _proj_kernel,
        grid=(t // tm, n // tn),
        in_specs=[pl.BlockSpec((tm, d), lambda i, j: (i, 0)),
                  pl.BlockSpec((1, 1, d), lambda i, j: (i // per_b, 0, 0)),
                  pl.BlockSpec((1, 1, d), lambda i, j: (i // per_b, 0, 0)),
                  pl.BlockSpec((d, tn), lambda i, j: (0, j))],
        out_specs=pl.BlockSpec((tm, tn), lambda i, j: (i, j)),
        out_shape=jax.ShapeDtypeStruct((t, n), F32),
        scratch_shapes=[pltpu.VMEM((tm, d), BF16)],
        compiler_params=_params(("parallel", "arbitrary")),
        name="proj",
    )(x2, sc, sh, w)


def _rglru_kernel(xr_ref, gr_ref, cw_ref, cb_ref, wa_ref, ba_ref, wx_ref, bx_ref, lam_ref,
                  o_ref, xbuf, abuf, bbuf, h_scr):
    ts, tc = xr_ref.shape
    pad = SUBLANES

    @pl.when(pl.program_id(2) == 0)
    def _():
        xbuf[0:pad, :] = jnp.zeros((pad, tc), F32)
        h_scr[...] = jnp.zeros_like(h_scr)

    xbuf[pad:pad + ts, :] = xr_ref[...]
    xc = cb_ref[...] + cw_ref[0:1, :] * xbuf[pad - 3:pad - 3 + ts, :]
    for k in range(1, CONV_WIDTH):
        off = pad - (CONV_WIDTH - 1) + k
        xc = xc + cw_ref[k:k + 1, :] * xbuf[off:off + ts, :]
    xbuf[0:pad, :] = xbuf[ts:ts + pad, :]

    xcb = xc.astype(BF16)
    ra, ia = [], []
    for hh in range(tc // LANES):
        xh = xcb[:, hh * LANES:(hh + 1) * LANES]
        ra.append(jnp.dot(xh, wa_ref[hh], preferred_element_type=F32))
        ia.append(jnp.dot(xh, wx_ref[hh], preferred_element_type=F32))
    r = jax.nn.sigmoid(jnp.concatenate(ra, axis=1) + ba_ref[...])
    ig = jax.nn.sigmoid(jnp.concatenate(ia, axis=1) + bx_ref[...])
    nlam = -lam_ref[...]
    softplus = jnp.maximum(nlam, 0.0) + jnp.log1p(jnp.exp(-jnp.abs(nlam)))
    log_a = (-LRU_C * r) * softplus
    a = jnp.exp(log_a)
    b = jnp.sqrt(-jnp.tanh(log_a) * (a * a + 1.0)) * (ig * xc)

    abuf[0:ts, :] = jnp.ones((ts, tc), F32)
    bbuf[0:ts, :] = jnp.zeros((ts, tc), F32)
    d = 1
    while d < ts:
        abuf[ts:2 * ts, :] = a
        bbuf[ts:2 * ts, :] = b
        a_sh = abuf[ts - d:2 * ts - d, :]
        b_sh = bbuf[ts - d:2 * ts - d, :]
        b = a * b_sh + b
        a = a * a_sh
        d *= 2
    h = a * h_scr[...] + b
    h_scr[...] = h[ts - 1:ts, :]
    o_ref[...] = (h * _gelu_tanh(gr_ref[...])).astype(o_ref.dtype)


def _rglru(proj3, col_xr, col_gr, conv_w, conv_b, wa, ba, wx, bx, lam):
    bsz, seq, _ = proj3.shape
    d_rnn = conv_w.shape[1]
    tc = min(512, d_rnn)
    ts = min(512, seq)
    nc = d_rnn // tc
    hpt = tc // LANES
    vec = lambda: pl.BlockSpec((1, tc), lambda b, c, s: (0, c))
    return pl.pallas_call(
        _rglru_kernel,
        grid=(bsz, nc, seq // ts),
        in_specs=[pl.BlockSpec((None, ts, tc), lambda b, c, s: (b, s, col_xr // tc + c)),
                  pl.BlockSpec((None, ts, tc), lambda b, c, s: (b, s, col_gr // tc + c)),
                  pl.BlockSpec((CONV_WIDTH, tc), lambda b, c, s: (0, c)),
                  vec(),
                  pl.BlockSpec((hpt, LANES, LANES), lambda b, c, s: (c, 0, 0)),
                  vec(),
                  pl.BlockSpec((hpt, LANES, LANES), lambda b, c, s: (c, 0, 0)),
                  vec(), vec()],
        out_specs=pl.BlockSpec((None, ts, tc), lambda b, c, s: (b, s, c)),
        out_shape=jax.ShapeDtypeStruct((bsz, seq, d_rnn), BF16),
        scratch_shapes=[pltpu.VMEM((ts + 2 * SUBLANES, tc), F32),
                        pltpu.VMEM((2 * ts, tc), F32),
                        pltpu.VMEM((2 * ts, tc), F32),
                        pltpu.VMEM((1, tc), F32)],
        compiler_params=_params(("parallel", "parallel", "arbitrary")),
        name="rglru",
    )(proj3, proj3, conv_w, conv_b, wa, ba, wx, bx, lam)


def _rope_kernel(pos_ref, invf_ref, cm_ref, sm_ref, cc_ref, ss_ref):
    ang = pos_ref[...].astype(F32) * invf_ref[...]
    cc_ref[...] = jnp.cos(ang) * cm_ref[...]
    ss_ref[...] = jnp.sin(ang) * sm_ref[...]


def _rope_tables(pos_col, invf, cmask, smask):
    t = pos_col.shape[0]
    tm = min(1024, t)
    row = lambda: pl.BlockSpec((1, LANES), lambda i: (0, 0))
    return pl.pallas_call(
        _rope_kernel,
        grid=(t // tm,),
        in_specs=[pl.BlockSpec((tm, 1), lambda i: (i, 0)), row(), row(), row()],
        out_specs=[pl.BlockSpec((tm, LANES), lambda i: (i, 0))] * 2,
        out_shape=[jax.ShapeDtypeStruct((t, LANES), F32)] * 2,
        compiler_params=_params(("parallel",)),
        name="rope",
    )(pos_col, invf, cmask, smask)


def _rms_bf16(x, g):
    return (x * lax.rsqrt(jnp.mean(x * x, axis=-1, keepdims=True) + RMS_EPS) * g).astype(BF16)


def _rope_lanes(r, cc, ss):
    return r * cc + pltpu.roll(r, LANES // 2, axis=1) * ss


def _qproj_kernel(qc_ref, g_ref, w_ref, cc_ref, ss_ref, o_ref, a_scr):
    @pl.when(pl.program_id(1) == 0)
    def _():
        a_scr[...] = _rms_bf16(qc_ref[...], g_ref[...])

    acc = jnp.dot(a_scr[...], w_ref[...], preferred_element_type=F32)
    qscale = (QK_NOPE + QK_ROPE) ** -0.5 * math.log2(math.e)
    o_ref[:, 0:QK_NOPE] = (acc[:, 0:QK_NOPE] * qscale).astype(o_ref.dtype)
    roped = _rope_lanes(acc[:, QK_NOPE:], cc_ref[...], ss_ref[...])
    o_ref[:, QK_NOPE:] = (roped * qscale).astype(o_ref.dtype)


def _qproj(proj, col_qc, g, w_ext, cc, ss, bsz, seq):
    t = proj.shape[0]
    nh, k, n = w_ext.shape
    tm = min(1024, seq)
    per_b = seq // tm
    return pl.pallas_call(
        _qproj_kernel,
        grid=(t // tm, nh),
        in_specs=[pl.BlockSpec((tm, k), lambda i, h: (i, col_qc // k)),
                  pl.BlockSpec((1, k), lambda i, h: (0, 0)),
                  pl.BlockSpec((None, k, n), lambda i, h: (h, 0, 0)),
                  pl.BlockSpec((tm, LANES), lambda i, h: (i, 0)),
                  pl.BlockSpec((tm, LANES), lambda i, h: (i, 0))],
        out_specs=pl.BlockSpec((None, None, tm, n), lambda i, h: (i // per_b, h, i % per_b, 0)),
        out_shape=jax.ShapeDtypeStruct((bsz, nh, seq, n), BF16),
        scratch_shapes=[pltpu.VMEM((tm, k), BF16)],
        compiler_params=_params(("parallel", "arbitrary")),
        name="qproj",
    )(proj, g, w_ext, cc, ss)


def _kvproj_kernel(kvc_ref, kr_ref, g_ref, w_ref, cc_ref, ss_ref, k_ref, v_ref, a_scr):
    @pl.when(pl.program_id(1) == 0)
    def _():
        a_scr[...] = _rms_bf16(kvc_ref[...], g_ref[...])

    acc = jnp.dot(a_scr[...], w_ref[...], preferred_element_type=F32)
    k_ref[:, 0:QK_NOPE] = acc[:, 0:QK_NOPE].astype(k_ref.dtype)
    k_ref[:, QK_NOPE:] = _rope_lanes(kr_ref[...], cc_ref[...], ss_ref[...]).astype(k_ref.dtype)
    v_ref[...] = acc[:, QK_NOPE:].astype(v_ref.dtype)


def _kvproj(proj, col_kvc, col_kr, g, w_h, cc, ss, bsz, seq):
    t = proj.shape[0]
    nh, k, n = w_h.shape
    tm = min(1024, seq)
    per_b = seq // tm
    omap = lambda i, h: (i // per_b, h, i % per_b, 0)
    return pl.pallas_call(
        _kvproj_kernel,
        grid=(t // tm, nh),
        in_specs=[pl.BlockSpec((tm, k), lambda i, h: (i, col_kvc // k)),
                  pl.BlockSpec((tm, LANES), lambda i, h: (i, col_kr // LANES)),
                  pl.BlockSpec((1, k), lambda i, h: (0, 0)),
                  pl.BlockSpec((None, k, n), lambda i, h: (h, 0, 0)),
                  pl.BlockSpec((tm, LANES), lambda i, h: (i, 0)),
                  pl.BlockSpec((tm, LANES), lambda i, h: (i, 0))],
        out_specs=[pl.BlockSpec((None, None, tm, 2 * LANES), omap),
                   pl.BlockSpec((None, None, tm, V_HEAD), omap)],
        out_shape=[jax.ShapeDtypeStruct((bsz, nh, seq, 2 * LANES), BF16),
                   jax.ShapeDtypeStruct((bsz, nh, seq, V_HEAD), BF16)],
        scratch_shapes=[pltpu.VMEM((tm, k), BF16)],
        compiler_params=_params(("parallel", "arbitrary")),
        name="kvproj",
    )(proj, proj, g, w_h, cc, ss)


ATTN_ROWS = 32


def _attn_kernel(qi_tab, ki_tab, q_ref, k_ref, v_ref, o_ref, m_scr, a_scr, acc_scr, s_scr, p_scr):
    step = pl.program_id(2)
    qi, ki = qi_tab[step], ki_tab[step]
    n_hd, tq, _ = q_ref.shape
    tk, dv = v_ref.shape[1], v_ref.shape[2]
    half = tq // 2
    n_chunks = tq // ATTN_ROWS

    @pl.when(ki == 0)
    def _():
        m_scr[...] = jnp.full_like(m_scr, NEG_INF)
        acc_scr[...] = jnp.zeros_like(acc_scr)

    def key_limit(hf, masked):
        return (hf + 1) * half if masked else tk

    def scores(hd, masked):
        for hf in range(2):
            hrows = slice(hf * half, (hf + 1) * half)
            klim = key_limit(hf, masked)
            s_scr[hd, hrows, 0:klim] = lax.dot_general(q_ref[hd, hrows, :], k_ref[hd, 0:klim, :],
                                                       (((1,), (1,)), ((), ())), preferred_element_type=F32)

    def softmax(hd, masked):
        def hidden(c, j):
            return masked and j * LANES > (c + 1) * ATTN_ROWS - 1

        def n_tiles(c):
            return key_limit(c * ATTN_ROWS // half, masked) // LANES

        for c in range(n_chunks):
            rows = slice(c * ATTN_ROWS, (c + 1) * ATTN_ROWS)
            tmax = None
            for j in range(n_tiles(c)):
                if hidden(c, j):
                    continue
                cols = slice(j * LANES, (j + 1) * LANES)
                s = s_scr[hd, rows, cols]
                if masked and (j + 1) * LANES - 1 > c * ATTN_ROWS:
                    row = c * ATTN_ROWS + lax.broadcasted_iota(jnp.int32, (ATTN_ROWS, LANES), 0)
                    col = j * LANES + lax.broadcasted_iota(jnp.int32, (ATTN_ROWS, LANES), 1)
                    s = jnp.where(col <= row, s, NEG_INF)
                    s_scr[hd, rows, cols] = s
                tmax = s if tmax is None else jnp.maximum(tmax, s)
            m_prev = m_scr[hd, rows, :]
            m_new = jnp.maximum(m_prev, jnp.max(tmax, axis=-1, keepdims=True))
            a_scr[hd, rows, :] = jnp.exp2(m_prev - m_new)
            m_scr[hd, rows, :] = m_new
        for c in range(n_chunks):
            rows = slice(c * ATTN_ROWS, (c + 1) * ATTN_ROWS)
            m_new = m_scr[hd, rows, :]
            for j in range(n_tiles(c)):
                cols = slice(j * LANES, (j + 1) * LANES)
                if hidden(c, j):
                    p_scr[hd, rows, cols] = jnp.zeros((ATTN_ROWS, LANES), BF16)
                else:
                    p_scr[hd, rows, cols] = jnp.exp2(s_scr[hd, rows, cols] - m_new).astype(BF16)

    def accumulate(hd, masked):
        v_ones = jnp.concatenate([v_ref[hd], jnp.ones((tk, dv), BF16)], axis=1)
        for hf in range(2):
            hrows = slice(hf * half, (hf + 1) * half)
            klim = key_limit(hf, masked)
            pv = jnp.dot(p_scr[hd, hrows, 0:klim], v_ones[0:klim, :], preferred_element_type=F32)
            alpha = a_scr[hd, hrows, :]
            acc_scr[hd, hrows, :] = (jnp.concatenate([alpha] * (2 * dv // LANES), axis=1) * acc_scr[hd, hrows, :]
                                     + pv)

    def block(masked):
        for hd in range(n_hd):
            scores(hd, masked)
        for hd in range(n_hd):
            softmax(hd, masked)
            accumulate(hd, masked)

    pl.when(ki < qi)(functools.partial(block, False))

    @pl.when(ki == qi)
    def _():
        block(True)
        for hd in range(n_hd):
            acc = acc_scr[hd]
            o_ref[:, hd * dv:(hd + 1) * dv] = (acc[:, :dv] / acc[:, dv:]).astype(o_ref.dtype)


def _attention(q, k, v):
    bsz, nh, seq, dk = q.shape
    dv = v.shape[-1]
    tq = tk = min(1024, seq)
    nq = seq // tq
    pairs = [(i, j) for i in range(nq) for j in range(i + 1)]
    qi_tab = jnp.asarray([p[0] for p in pairs], jnp.int32)
    ki_tab = jnp.asarray([p[1] for p in pairs], jnp.int32)
    hp = 2 if nh % 2 == 0 else 1
    grid_spec = pltpu.PrefetchScalarGridSpec(
        num_scalar_prefetch=2,
        grid=(bsz, nh // hp, len(pairs)),
        in_specs=[pl.BlockSpec((None, hp, tq, dk), lambda b, h, s, qt, kt: (b, h, qt[s], 0)),
                  pl.BlockSpec((None, hp, tk, dk), lambda b, h, s, qt, kt: (b, h, kt[s], 0)),
                  pl.BlockSpec((None, hp, tk, dv), lambda b, h, s, qt, kt: (b, h, kt[s], 0))],
        out_specs=pl.BlockSpec((None, tq, hp * dv), lambda b, h, s, qt, kt: (b, qt[s], h)),
        scratch_shapes=[pltpu.VMEM((hp, tq, LANES), F32), pltpu.VMEM((hp, tq, LANES), F32),
                        pltpu.VMEM((hp, tq, 2 * dv), F32), pltpu.VMEM((hp, tq, tk), F32),
                        pltpu.VMEM((hp, tq, tk), BF16)],
    )
    return pl.pallas_call(
        _attn_kernel,
        grid_spec=grid_spec,
        out_shape=jax.ShapeDtypeStruct((bsz, seq, nh * dv), BF16),
        compiler_params=_params(("parallel", "parallel", "arbitrary")),
        name="attn",
    )(qi_tab, ki_tab, q, k, v)


def _mix_kernel(hr_ref, o_ref, grnn_ref, gmla_ref, x_ref, gt_ref, w1_ref, w2_ref, wo_ref, lg_ref, lb_ref,
                out_ref, *, alpha):
    y_rnn = jnp.dot(hr_ref[...], w1_ref[...], preferred_element_type=F32)
    y_mla = jnp.dot(o_ref[...], w2_ref[...], preferred_element_type=F32)
    mixed = jax.nn.sigmoid(grnn_ref[...]) * y_rnn + jax.nn.sigmoid(gmla_ref[...]) * y_mla
    z = jnp.dot(mixed.astype(BF16), wo_ref[...], preferred_element_type=F32)
    z = alpha * x_ref[...] + (1.0 + gt_ref[0]) * z
    out_ref[...] = _layer_norm(z, lg_ref[...], lb_ref[...])


def _mix(hr, o, proj, col_grnn, col_gmla, x2, gt, w1, w2, wo, lg, lb, seq, alpha):
    t, d = x2.shape
    tm = min(256, seq)
    per_b = seq // tm
    rows = lambda width, cb=0: pl.BlockSpec((tm, width), lambda i: (i, cb))
    wspec = lambda w: _resident(w.shape, lambda i: (0, 0))
    return pl.pallas_call(
        functools.partial(_mix_kernel, alpha=alpha),
        grid=(t // tm,),
        in_specs=[rows(hr.shape[1]), rows(o.shape[1]),
                  rows(d, col_grnn // d), rows(d, col_gmla // d), rows(d),
                  pl.BlockSpec((1, 1, d), lambda i: (i // per_b, 0, 0)),
                  wspec(w1), wspec(w2), wspec(wo),
                  _resident((1, d), lambda i: (0, 0)), _resident((1, d), lambda i: (0, 0))],
        out_specs=rows(d),
        out_shape=jax.ShapeDtypeStruct((t, d), F32),
        compiler_params=_params(("parallel",)),
        name="mix",
    )(hr, o, proj, proj, x2, gt, w1, w2, wo, lg, lb)


def _topk_rows(s, k):
    n, tm = s.shape
    iota = lax.broadcasted_iota(jnp.int32, (n, tm), 0).astype(F32)
    krow = lax.broadcasted_iota(jnp.int32, (k, tm), 0)
    ts = jnp.zeros((k, tm), F32)
    ti = jnp.zeros((k, tm), F32)
    for kk in range(k):
        m = jnp.max(s, axis=0, keepdims=True)
        idx = jnp.min(jnp.where(s == m, iota, float(n)), axis=0, keepdims=True)
        ts = jnp.where(krow == kk, m, ts)
        ti = jnp.where(krow == kk, idx, ti)
        s = jnp.where(iota == idx, NEG_INF, s)
    return ts, ti


def _pair_topk(ts1, ti1, ts2, ti2, n_keys):
    k, tm = ts1.shape
    row8 = lax.broadcasted_iota(jnp.int32, (SUBLANES, tm), 0)
    rowk = lax.broadcasted_iota(jnp.int32, (k, tm), 0)
    cs, fs, es = [], [], []
    for i in range(k):
        nj = k // (i + 1)
        rows, jr = (k, rowk) if nj > SUBLANES else (SUBLANES, row8)
        c = ts1[i:i + 1, :] + ts2[0:rows, :]
        e = ti1[i:i + 1, :] * float(n_keys) + ti2[0:rows, :]
        cs.append(jnp.where(jr < nj, c, NEG_INF))
        fs.append((jr + i * k).astype(F32))
        es.append(e)
    cand = jnp.concatenate(cs, axis=0)
    flat = jnp.concatenate(fs, axis=0)
    expert = jnp.concatenate(es, axis=0)
    best_s = jnp.zeros((k, tm), F32)
    best_e = jnp.zeros((k, tm), F32)
    for kk in range(k):
        m = jnp.max(cand, axis=0, keepdims=True)
        fi = jnp.min(jnp.where(cand == m, flat, float(k * k)), axis=0, keepdims=True)
        sel = flat == fi
        ev = jnp.sum(jnp.where(sel, expert, 0.0), axis=0, keepdims=True)
        best_s = jnp.where(rowk == kk, m, best_s)
        best_e = jnp.where(rowk == kk, ev, best_e)
        cand = jnp.where(sel, NEG_INF, cand)
    return best_s, best_e


def _route_kernel(x_ref, sc_ref, sh_ref, wq_ref, keys_ref, h2_ref, e_ref, g_ref, q_scr, e_scr, g_scr,
                  *, n_heads):
    n_keys = keys_ref.shape[1]
    dk = keys_ref.shape[2]
    h2 = (x_ref[...] * (1.0 + sc_ref[0]) + sh_ref[0]).astype(BF16)
    h2_ref[...] = h2
    q = jnp.dot(h2, wq_ref[...], preferred_element_type=F32)
    for c in range(2 * n_heads):
        q_scr[c] = q[:, c * dk:(c + 1) * dk].astype(BF16)

    def head_body(h, carry):
        tops = []
        for p in range(2):
            s = lax.dot_general(keys_ref[p], q_scr[2 * h + p], (((1,), (1,)), ((), ())),
                                preferred_element_type=F32)
            tops.extend(_topk_rows(s, PEER_TOPK))
        best_s, best_e = _pair_topk(*tops, n_keys)
        ex = jnp.exp(best_s - jnp.max(best_s, axis=0, keepdims=True))
        gate = ex / jnp.sum(ex, axis=0, keepdims=True)
        r0 = pl.multiple_of(h * PEER_TOPK, PEER_TOPK)
        e_scr[pl.ds(r0, PEER_TOPK), :] = best_e
        g_scr[pl.ds(r0, PEER_TOPK), :] = gate
        return carry

    lax.fori_loop(0, n_heads, head_body, 0)
    e_ref[...] = e_scr[...].T
    g_ref[...] = g_scr[...].T


def _route(x1, sc, sh, wq, keys, seq):
    t, d = x1.shape
    nq = wq.shape[1]
    dk = keys.shape[2]
    n_heads = nq // (2 * dk)
    slots = n_heads * PEER_TOPK
    tm = min(256, seq)
    per_b = seq // tm
    return pl.pallas_call(
        functools.partial(_route_kernel, n_heads=n_heads),
        grid=(t // tm,),
        in_specs=[pl.BlockSpec((tm, d), lambda i: (i, 0)),
                  pl.BlockSpec((1, 1, d), lambda i: (i // per_b, 0, 0)),
                  pl.BlockSpec((1, 1, d), lambda i: (i // per_b, 0, 0)),
                  _resident(wq.shape, lambda i: (0, 0)),
                  _resident(keys.shape, lambda i: (0, 0, 0))],
        out_specs=[pl.BlockSpec((tm, d), lambda i: (i, 0)),
                   pl.BlockSpec((tm, slots), lambda i: (i, 0)),
                   pl.BlockSpec((tm, slots), lambda i: (i, 0))],
        out_shape=[jax.ShapeDtypeStruct((t, d), BF16),
                   jax.ShapeDtypeStruct((t, slots), F32),
                   jax.ShapeDtypeStruct((t, slots), F32)],
        scratch_shapes=[pltpu.VMEM((2 * n_heads, tm, dk), BF16),
                        pltpu.VMEM((slots, tm), F32),
                        pltpu.VMEM((slots, tm), F32)],
        compiler_params=_params(("parallel",)),
        name="route",
    )(x1, sc, sh, wq, keys)


GB_TOKENS = 128
GB_PITCH = GB_TOKENS + SUBLANES
GB_UNROLL = 32


def _gbuild_kernel(e_ref, g_ref, o_ref, *, n_keys):
    tmb, slots = e_ref.shape
    sub = lax.broadcasted_iota(jnp.int32, (n_keys, slots), 0).astype(F32)
    zeros = jnp.zeros((n_keys, n_keys), F32)
    for r in range(tmb, GB_PITCH):
        o_ref[pl.ds(r, n_keys, stride=GB_PITCH), :] = zeros

    def one_hots(t):
        e = e_ref[pl.ds(t, 1), :]
        g = g_ref[pl.ds(t, 1), :]
        a = jnp.floor(e * (1.0 / n_keys))
        b = e - a * float(n_keys)
        p1 = jnp.where(sub == a, g, 0.0).astype(BF16)
        p2 = jnp.where(sub == b, 1.0, 0.0).astype(BF16)
        return p1, p2

    zpad = jnp.zeros((n_keys, slots), BF16)

    def body(tb, carry):
        for u in range(0, GB_UNROLL, 2):
            ta = tb * GB_UNROLL + u
            p1a, p2a = one_hots(ta)
            p1b, p2b = one_hots(ta + 1)
            lhs = jnp.concatenate([p1a, p1b], axis=1)
            rhs = jnp.concatenate([jnp.concatenate([p2a, zpad], axis=1),
                                   jnp.concatenate([zpad, p2b], axis=1)], axis=0)
            grids = lax.dot_general(lhs, rhs, (((1,), (1,)), ((), ())), preferred_element_type=F32)
            o_ref[pl.ds(ta, n_keys, stride=GB_PITCH), :] = grids[:, 0:n_keys]
            o_ref[pl.ds(ta + 1, n_keys, stride=GB_PITCH), :] = grids[:, n_keys:]
        return carry

    lax.fori_loop(0, tmb // GB_UNROLL, body, 0)


def _gbuild(e, g, n_keys):
    t, slots = e.shape
    tmb = GB_TOKENS
    return pl.pallas_call(
        functools.partial(_gbuild_kernel, n_keys=n_keys),
        grid=(t // tmb,),
        in_specs=[pl.BlockSpec((tmb, slots), lambda i: (i, 0))] * 2,
        out_specs=pl.BlockSpec((None, n_keys * GB_PITCH, n_keys), lambda i: (i, 0, 0)),
        out_shape=jax.ShapeDtypeStruct((t // tmb, n_keys * GB_PITCH, n_keys), F32),
        compiler_params=_params(("parallel",)),
        name="gbuild",
    )(e, g)


def _peer_kernel(h2_ref, ut_ref, v_ref, g_ref, x1_ref, gt_ref, lg_ref, lb_ref, o_ref, coef_scr, *, alpha):
    j = pl.program_id(1)
    n_tiles = pl.num_programs(1) - 1
    _, n1, _, n_keys = g_ref.shape
    tm = h2_ref.shape[0]

    def score_tile():
        a = jnp.dot(h2_ref[...], ut_ref[...], preferred_element_type=F32)
        slot = j % 2
        for k in range(n1):
            cols = slice(k * n_keys, (k + 1) * n_keys)
            gate = g_ref[:, k, 0:GB_TOKENS, :].reshape(tm, n_keys)
            coef_scr[slot, :, cols] = (_gelu_tanh(a[:, cols]) * gate).astype(BF16)

    def accumulate():
        o_ref[...] += jnp.dot(coef_scr[(j + 1) % 2], v_ref[...], preferred_element_type=F32)

    @pl.when(j == 0)
    def _():
        o_ref[...] = jnp.zeros_like(o_ref)
        score_tile()

    @pl.when(jnp.logical_and(j > 0, j < n_tiles))
    def _():
        accumulate()
        score_tile()

    @pl.when(j == n_tiles)
    def _():
        accumulate()
        z = alpha * x1_ref[...] + (1.0 + gt_ref[0]) * o_ref[...]
        o_ref[...] = _layer_norm(z, lg_ref[...], lb_ref[...])


def _peer(h2, ut, v, g4, x1, gt, lg, lb, seq, alpha):
    t, d = h2.shape
    ne = v.shape[0]
    n_keys = g4.shape[3]
    tm = min(512, seq)
    te = 1024
    n1 = te // n_keys
    n_tiles = ne // te
    per_b = seq // tm
    cur = lambda j: jnp.minimum(j, n_tiles - 1)
    prev = lambda j: jnp.maximum(j - 1, 0)
    return pl.pallas_call(
        functools.partial(_peer_kernel, alpha=alpha),
        grid=(t // tm, n_tiles + 1),
        in_specs=[pl.BlockSpec((tm, d), lambda i, j: (i, 0)),
                  pl.BlockSpec((d, te), lambda i, j: (0, cur(j))),
                  pl.BlockSpec((te, d), lambda i, j: (prev(j), 0)),
                  pl.BlockSpec((tm // GB_TOKENS, n1, GB_PITCH, n_keys), lambda i, j: (i, cur(j), 0, 0)),
                  pl.BlockSpec((tm, d), lambda i, j: (i, 0)),
                  pl.BlockSpec((1, 1, d), lambda i, j: (i // per_b, 0, 0)),
                  pl.BlockSpec((1, d), lambda i, j: (0, 0)),
                  pl.BlockSpec((1, d), lambda i, j: (0, 0))],
        out_specs=pl.BlockSpec((tm, d), lambda i, j: (i, 0)),
        out_shape=jax.ShapeDtypeStruct((t, d), F32),
        scratch_shapes=[pltpu.VMEM((2, tm, te), BF16)],
        compiler_params=_params(("parallel", "arbitrary")),
        name="peer",
    )(h2, ut, v, g4, x1, gt, lg, lb)


def _swap_halves(w):
    half = w.shape[-1] // 2
    return jnp.concatenate([w[..., half:], w[..., :half]], axis=-1)


def _layer(x2, mod, cc, ss, bsz, seq, depth, p):
    t, d = x2.shape
    sh1, sc1, gt1, sh2, sc2, gt2 = [m[:, None, :] for m in jnp.split(mod, 6, axis=-1)]
    alpha = (2.0 * depth) ** 0.25
    d_rnn = p["conv_w"].shape[1]
    q_lora = p["q_norm_g"].shape[0]
    kv_lora = p["kv_norm_g"].shape[0]
    n_heads = p["w_uq"].shape[1] // (QK_NOPE + QK_ROPE)

    w_in = p["w_in"]
    widths = (d_rnn, d_rnn, q_lora, kv_lora, QK_ROPE, d, d)
    offs = [0]
    for wd in widths:
        offs.append(offs[-1] + wd)
    pieces = [w_in[:, offs[i]:offs[i + 1]] for i in range(7)]
    w_xr, w_gr, w_qc, w_kvc, w_kr, w_grnn, w_gmla = pieces
    kr_pad = 512
    w_kr_ext = jnp.concatenate([w_kr, _swap_halves(w_kr),
                                jnp.zeros((d, kr_pad - 2 * QK_ROPE), w_in.dtype)], axis=1)
    w_in_p = jnp.concatenate([w_xr, w_gr, w_grnn, w_gmla, w_qc, w_kvc, w_kr_ext], axis=1).astype(BF16)
    col_xr, col_gr, col_grnn, col_gmla = 0, d_rnn, 2 * d_rnn, 2 * d_rnn + d
    col_qc = 2 * d_rnn + 2 * d
    col_kvc = col_qc + q_lora
    col_kr = col_kvc + kv_lora

    proj = _proj(x2, sc1, sh1, w_in_p, seq)
    proj3 = proj.reshape(bsz, seq, proj.shape[1])

    row = lambda v: v[None, :]
    hr = _rglru(proj3, col_xr, col_gr, p["conv_w"], row(p["conv_b"]),
                p["w_rg_a"].astype(BF16), row(p["b_rg_a"]), p["w_rg_x"].astype(BF16), row(p["b_rg_x"]),
                row(p["rg_lambda"]))
    hr = hr.reshape(t, d_rnn)

    w_uq = p["w_uq"].reshape(q_lora, n_heads, QK_NOPE + QK_ROPE)
    w_rope = w_uq[..., QK_NOPE:]
    w_uq_ext = jnp.concatenate([w_uq[..., :QK_NOPE], w_rope, _swap_halves(w_rope)], axis=-1)
    w_uq_ext = w_uq_ext.transpose(1, 0, 2).astype(BF16)
    w_ukv = p["w_ukv"].reshape(kv_lora, n_heads, QK_NOPE + V_HEAD).transpose(1, 0, 2).astype(BF16)
    q = _qproj(proj, col_qc, row(p["q_norm_g"]), w_uq_ext, cc, ss, bsz, seq)
    k, v = _kvproj(proj, col_kvc, col_kr, row(p["kv_norm_g"]), w_ukv, cc, ss, bsz, seq)
    o = _attention(q, k, v)
    o = o.reshape(t, n_heads * V_HEAD)

    x1 = _mix(hr, o, proj, col_grnn, col_gmla, x2, gt1,
              p["w_rnn_out"].astype(BF16), p["w_mla_out"].astype(BF16), p["w_o"].astype(BF16),
              row(p["ln1_g"]), row(p["ln1_b"]), seq, alpha)

    keys = p["peer_keys"]
    n_keys = keys.shape[1]
    h2, e, g = _route(x1, sc2, sh2, p["peer_wq"].astype(BF16), keys.astype(BF16), seq)
    g3 = _gbuild(e, g, n_keys)
    g4 = g3.reshape(g3.shape[0], n_keys, GB_PITCH, n_keys)
    return _peer(h2, p["peer_u"].astype(BF16).T, p["peer_v"].astype(BF16), g4, x1, gt2,
                 row(p["ln2_g"]), row(p["ln2_b"]), seq, alpha)


def kernel(x, c, positions, w_ada, b_ada, w_in, conv_w, conv_b, w_rg_a, b_rg_a, w_rg_x, b_rg_x, rg_lambda,
           w_rnn_out, q_norm_g, w_uq, kv_norm_g, w_ukv, w_mla_out, w_o, ln1_g, ln1_b, peer_wq, peer_keys,
           peer_u, peer_v, ln2_g, ln2_b):
    bsz, seq, d = x.shape
    depth = w_ada.shape[0]
    t = bsz * seq
    stacked = dict(w_in=w_in, conv_w=conv_w, conv_b=conv_b, w_rg_a=w_rg_a, b_rg_a=b_rg_a, w_rg_x=w_rg_x,
                   b_rg_x=b_rg_x, rg_lambda=rg_lambda, w_rnn_out=w_rnn_out, q_norm_g=q_norm_g, w_uq=w_uq,
                   kv_norm_g=kv_norm_g, w_ukv=w_ukv, w_mla_out=w_mla_out, w_o=w_o, ln1_g=ln1_g, ln1_b=ln1_b,
                   peer_wq=peer_wq, peer_keys=peer_keys, peer_u=peer_u, peer_v=peer_v, ln2_g=ln2_g,
                   ln2_b=ln2_b)

    half = QK_ROPE // 2
    inv_freq = ROPE_THETA ** (-jnp.arange(0, QK_ROPE, 2, dtype=F32) / QK_ROPE)
    zeros = jnp.zeros((half,), F32)
    ones = jnp.ones((half,), F32)
    invf = jnp.concatenate([inv_freq, inv_freq, zeros, zeros])[None, :]
    cmask = jnp.concatenate([ones, ones, zeros, zeros])[None, :]
    smask = jnp.concatenate([-ones, ones, zeros, zeros])[None, :]
    cc, ss = _rope_tables(positions.reshape(t, 1), invf, cmask, smask)

    c_pad = jnp.concatenate([c, jnp.zeros((SUBLANES - bsz % SUBLANES, d), c.dtype)], axis=0)
    x2 = x.reshape(t, d)
    for l in range(depth):
        mod = _ada(c_pad, w_ada[l], b_ada[l][None, :])[:bsz]
        x2 = _layer(x2, mod, cc, ss, bsz, seq, depth, {k: v[l] for k, v in stacked.items()})
    return x2.reshape(bsz, seq, d)
```

```python
import functools
import math

import jax
import jax.numpy as jnp
from jax import lax
from jax.experimental import pallas as pl
from jax.experimental.pallas import tpu as pltpu

F32 = jnp.float32
BF16 = jnp.bfloat16

CONV_WIDTH = 4
LRU_C = 8.0
QK_NOPE = 128
QK_ROPE = 64
V_HEAD = 128
ROPE_THETA = 10000.0
PEER_TOPK = 16
LN_EPS = 1e-5
RMS_EPS = 1e-6
LANES = 128
SUBLANES = 8
VMEM_LIMIT = 56 * 1024 * 1024
NEG_INF = float("-inf")


def _params(sem):
    return pltpu.CompilerParams(dimension_semantics=sem, vmem_limit_bytes=VMEM_LIMIT)


def _resident(shape, index_map):
    return pl.BlockSpec(shape, index_map, pipeline_mode=pl.Buffered(1))


def _gelu_tanh(x):
    return 0.5 * x * (1.0 + jnp.tanh(math.sqrt(2.0 / math.pi) * (x + 0.044715 * (x * x * x))))


def _layer_norm(z, g, b):
    mu = jnp.mean(z, axis=-1, keepdims=True)
    d = z - mu
    var = jnp.mean(d * d, axis=-1, keepdims=True)
    return d * lax.rsqrt(var + LN_EPS) * g + b


def _ada_kernel(c_ref, w_ref, b_ref, o_ref):
    c = c_ref[...]
    ca = c * jax.nn.sigmoid(c)
    o_ref[...] = jnp.dot(ca, w_ref[...], preferred_element_type=F32,
                         precision=lax.Precision.HIGHEST) + b_ref[...]


def _ada(c_pad, w, b):
    m, d = c_pad.shape
    n = w.shape[1]
    tn = min(n, 1024)
    return pl.pallas_call(
        _ada_kernel,
        grid=(n // tn,),
        in_specs=[pl.BlockSpec((m, d), lambda j: (0, 0)),
                  pl.BlockSpec((d, tn), lambda j: (0, j)),
                  pl.BlockSpec((1, tn), lambda j: (0, j))],
        out_specs=pl.BlockSpec((m, tn), lambda j: (0, j)),
        out_shape=jax.ShapeDtypeStruct((m, n), F32),
        compiler_params=_params(("arbitrary",)),
        name="ada",
    )(c_pad, w, b)


def _proj_kernel(x_ref, sc_ref, sh_ref, w_ref, o_ref, a_scr):
    @pl.when(pl.program_id(1) == 0)
    def _():
        a_scr[...] = (x_ref[...] * (1.0 + sc_ref[0]) + sh_ref[0]).astype(BF16)

    o_ref[...] = jnp.dot(a_scr[...], w_ref[...], preferred_element_type=F32)


def _proj(x2, sc, sh, w, seq):
    t, d = x2.shape
    n = w.shape[1]
    tm = min(1024, seq)
    tn = 512
    per_b = seq // tm
    return pl.pallas_call(
        _proj_kernel,
        grid=(t // tm, n // tn),
        in_specs=[pl.BlockSpec((tm, d), lambda i, j: (i, 0)),
                  pl.BlockSpec((1, 1, d), lambda i, j: (i // per_b, 0, 0)),
                  pl.BlockSpec((1, 1, d), lambda i, j: (i // per_b, 0, 0)),
                  pl.BlockSpec((d, tn), lambda i, j: (0, j))],
        out_specs=pl.BlockSpec((tm, tn), lambda i, j: (i, j)),
        out_shape=jax.ShapeDtypeStruct((t, n), F32),
        scratch_shapes=[pltpu.VMEM((tm, d), BF16)],
        compiler_params=_params(("parallel", "arbitrary")),
        name="proj",
    )(x2, sc, sh, w)


def _rglru_kernel(xr_ref, gr_ref, cw_ref, cb_ref, wa_ref, ba_ref, wx_ref, bx_ref, lam_ref,
                  o_ref, xbuf, abuf, bbuf, h_scr):
    ts, tc = xr_ref.shape
    pad = SUBLANES

    @pl.when(pl.program_id(2) == 0)
    def _():
        xbuf[0:pad, :] = jnp.zeros((pad, tc), F32)
        h_scr[...] = jnp.zeros_like(h_scr)

    xbuf[pad:pad + ts, :] = xr_ref[...]
    xc = cb_ref[...] + cw_ref[0:1, :] * xbuf[pad - 3:pad - 3 + ts, :]
    for k in range(1, CONV_WIDTH):
        off = pad - (CONV_WIDTH - 1) + k
        xc = xc + cw_ref[k:k + 1, :] * xbuf[off:off + ts, :]
    xbuf[0:pad, :] = xbuf[ts:ts + pad, :]

    xcb = xc.astype(BF16)
    ra, ia = [], []
    for hh in range(tc // LANES):
        xh = xcb[:, hh * LANES:(hh + 1) * LANES]
        ra.append(jnp.dot(xh, wa_ref[hh], preferred_element_type=F32))
        ia.append(jnp.dot(xh, wx_ref[hh], preferred_element_type=F32))
    r = jax.nn.sigmoid(jnp.concatenate(ra, axis=1) + ba_ref[...])
    ig = jax.nn.sigmoid(jnp.concatenate(ia, axis=1) + bx_ref[...])
    nlam = -lam_ref[...]
    softplus = jnp.maximum(nlam, 0.0) + jnp.log1p(jnp.exp(-jnp.abs(nlam)))
    log_a = (-LRU_C * r) * softplus
    a = jnp.exp(log_a)
    b = jnp.sqrt(-jnp.tanh(log_a) * (a * a + 1.0)) * (ig * xc)

    abuf[0:ts, :] = jnp.ones((ts, tc), F32)
    bbuf[0:ts, :] = jnp.zeros((ts, tc), F32)
    d = 1
    while d < ts:
        abuf[ts:2 * ts, :] = a
        bbuf[ts:2 * ts, :] = b
        a_sh = abuf[ts - d:2 * ts - d, :]
        b_sh = bbuf[ts - d:2 * ts - d, :]
        b = a * b_sh + b
        a = a * a_sh
        d *= 2
    h = a * h_scr[...] + b
    h_scr[...] = h[ts - 1:ts, :]
    o_ref[...] = (h * _gelu_tanh(gr_ref[...])).astype(o_ref.dtype)


def _rglru(proj3, col_xr, col_gr, conv_w, conv_b, wa, ba, wx, bx, lam):
    bsz, seq, _ = proj3.shape
    d_rnn = conv_w.shape[1]
    tc = min(512, d_rnn)
    ts = min(512, seq)
    nc = d_rnn // tc
    hpt = tc // LANES
    vec = lambda: pl.BlockSpec((1, tc), lambda b, c, s: (0, c))
    return pl.pallas_call(
        _rglru_kernel,
        grid=(bsz, nc, seq // ts),
        in_specs=[pl.BlockSpec((None, ts, tc), lambda b, c, s: (b, s, col_xr // tc + c)),
                  pl.BlockSpec((None, ts, tc), lambda b, c, s: (b, s, col_gr // tc + c)),
                  pl.BlockSpec((CONV_WIDTH, tc), lambda b, c, s: (0, c)),
                  vec(),
                  pl.BlockSpec((hpt, LANES, LANES), lambda b, c, s: (c, 0, 0)),
                  vec(),
                  pl.BlockSpec((hpt, LANES, LANES), lambda b, c, s: (c, 0, 0)),
                  vec(), vec()],
        out_specs=pl.BlockSpec((None, ts, tc), lambda b, c, s: (b, s, c)),
        out_shape=jax.ShapeDtypeStruct((bsz, seq, d_rnn), BF16),
        scratch_shapes=[pltpu.VMEM((ts + 2 * SUBLANES, tc), F32),
                        pltpu.VMEM((2 * ts, tc), F32),
                        pltpu.VMEM((2 * ts, tc), F32),
                        pltpu.VMEM((1, tc), F32)],
        compiler_params=_params(("parallel", "parallel", "arbitrary")),
        name="rglru",
    )(proj3, proj3, conv_w, conv_b, wa, ba, wx, bx, lam)


def _rope_kernel(pos_ref, invf_ref, cm_ref, sm_ref, cc_ref, ss_ref):
    ang = pos_ref[...].astype(F32) * invf_ref[...]
    cc_ref[...] = jnp.cos(ang) * cm_ref[...]
    ss_ref[...] = jnp.sin(ang) * sm_ref[...]


def _rope_tables(pos_col, invf, cmask, smask):
    t = pos_col.shape[0]
    tm = min(1024, t)
    row = lambda: pl.BlockSpec((1, LANES), lambda i: (0, 0))
    return pl.pallas_call(
        _rope_kernel,
        grid=(t // tm,),
        in_specs=[pl.BlockSpec((tm, 1), lambda i: (i, 0)), row(), row(), row()],
        out_specs=[pl.BlockSpec((tm, LANES), lambda i: (i, 0))] * 2,
        out_shape=[jax.ShapeDtypeStruct((t, LANES), F32)] * 2,
        compiler_params=_params(("parallel",)),
        name="rope",
    )(pos_col, invf, cmask, smask)


def _rms_bf16(x, g):
    return (x * lax.rsqrt(jnp.mean(x * x, axis=-1, keepdims=True) + RMS_EPS) * g).astype(BF16)


def _rope_lanes(r, cc, ss):
    return r * cc + pltpu.roll(r, LANES // 2, axis=1) * ss


PROJ_HEADS = 4


def _qproj_kernel(qc_ref, g_ref, w_ref, cc_ref, ss_ref, o_ref, a_scr):
    @pl.when(pl.program_id(1) == 0)
    def _():
        a_scr[...] = _rms_bf16(qc_ref[...], g_ref[...])

    qscale = (QK_NOPE + QK_ROPE) ** -0.5 * math.log2(math.e)
    a = a_scr[...]
    for hd in range(w_ref.shape[0]):
        acc = jnp.dot(a, w_ref[hd], preferred_element_type=F32)
        o_ref[hd, :, 0:QK_NOPE] = (acc[:, 0:QK_NOPE] * qscale).astype(o_ref.dtype)
        roped = _rope_lanes(acc[:, QK_NOPE:], cc_ref[...], ss_ref[...])
        o_ref[hd, :, QK_NOPE:] = (roped * qscale).astype(o_ref.dtype)


def _heads_per_step(nh):
    return PROJ_HEADS if nh % PROJ_HEADS == 0 else 1


def _qproj(proj, col_qc, g, w_ext, cc, ss, bsz, seq):
    t = proj.shape[0]
    nh, k, n = w_ext.shape
    hp = _heads_per_step(nh)
    tm = min(1024, seq)
    per_b = seq // tm
    return pl.pallas_call(
        _qproj_kernel,
        grid=(t // tm, nh // hp),
        in_specs=[pl.BlockSpec((tm, k), lambda i, h: (i, col_qc // k)),
                  pl.BlockSpec((1, k), lambda i, h: (0, 0)),
                  pl.BlockSpec((hp, k, n), lambda i, h: (h, 0, 0)),
                  pl.BlockSpec((tm, LANES), lambda i, h: (i, 0)),
                  pl.BlockSpec((tm, LANES), lambda i, h: (i, 0))],
        out_specs=pl.BlockSpec((None, hp, tm, n), lambda i, h: (i // per_b, h, i % per_b, 0)),
        out_shape=jax.ShapeDtypeStruct((bsz, nh, seq, n), BF16),
        scratch_shapes=[pltpu.VMEM((tm, k), BF16)],
        compiler_params=_params(("parallel", "arbitrary")),
        name="qproj",
    )(proj, g, w_ext, cc, ss)


def _kvproj_kernel(kvc_ref, kr_ref, g_ref, w_ref, cc_ref, ss_ref, k_ref, v_ref, a_scr, kr_scr):
    @pl.when(pl.program_id(1) == 0)
    def _():
        a_scr[...] = _rms_bf16(kvc_ref[...], g_ref[...])
        kr_scr[...] = _rope_lanes(kr_ref[...], cc_ref[...], ss_ref[...]).astype(kr_scr.dtype)

    a = a_scr[...]
    for hd in range(w_ref.shape[0]):
        acc = jnp.dot(a, w_ref[hd], preferred_element_type=F32)
        k_ref[hd, :, 0:QK_NOPE] = acc[:, 0:QK_NOPE].astype(k_ref.dtype)
        k_ref[hd, :, QK_NOPE:] = kr_scr[...]
        v_ref[hd] = acc[:, QK_NOPE:].astype(v_ref.dtype)


def _kvproj(proj, col_kvc, col_kr, g, w_h, cc, ss, bsz, seq):
    t = proj.shape[0]
    nh, k, n = w_h.shape
    hp = _heads_per_step(nh)
    tm = min(1024, seq)
    per_b = seq // tm
    omap = lambda i, h: (i // per_b, h, i % per_b, 0)
    return pl.pallas_call(
        _kvproj_kernel,
        grid=(t // tm, nh // hp),
        in_specs=[pl.BlockSpec((tm, k), lambda i, h: (i, col_kvc // k)),
                  pl.BlockSpec((tm, LANES), lambda i, h: (i, col_kr // LANES)),
                  pl.BlockSpec((1, k), lambda i, h: (0, 0)),
                  pl.BlockSpec((hp, k, n), lambda i, h: (h, 0, 0)),
                  pl.BlockSpec((tm, LANES), lambda i, h: (i, 0)),
                  pl.BlockSpec((tm, LANES), lambda i, h: (i, 0))],
        out_specs=[pl.BlockSpec((None, hp, tm, 2 * LANES), omap),
                   pl.BlockSpec((None, hp, tm, V_HEAD), omap)],
        out_shape=[jax.ShapeDtypeStruct((bsz, nh, seq, 2 * LANES), BF16),
                   jax.ShapeDtypeStruct((bsz, nh, seq, V_HEAD), BF16)],
        scratch_shapes=[pltpu.VMEM((tm, k), BF16), pltpu.VMEM((tm, LANES), BF16)],
        compiler_params=_params(("parallel", "arbitrary")),
        name="kvproj",
    )(proj, proj, g, w_h, cc, ss)


ATTN_ROWS = 32


def _attn_kernel(qi_tab, ki_tab, q_ref, k_ref, v_ref, o_ref, m_scr, a_scr, acc_scr, s_scr, p_scr):
    step = pl.program_id(2)
    qi, ki = qi_tab[step], ki_tab[step]
    n_hd, tq, _ = q_ref.shape
    tk, dv = v_ref.shape[1], v_ref.shape[2]
    half = tq // 2
    n_chunks = tq // ATTN_ROWS

    @pl.when(ki == 0)
    def _():
        m_scr[...] = jnp.full_like(m_scr, NEG_INF)
        acc_scr[...] = jnp.zeros_like(acc_scr)

    def key_limit(hf, masked):
        return (hf + 1) * half if masked else tk

    def scores(hd, masked):
        for hf in range(2):
            hrows = slice(hf * half, (hf + 1) * half)
            klim = key_limit(hf, masked)
            s_scr[hd, hrows, 0:klim] = lax.dot_general(q_ref[hd, hrows, :], k_ref[hd, 0:klim, :],
                                                       (((1,), (1,)), ((), ())), preferred_element_type=F32)

    def softmax(hd, masked):
        def hidden(c, j):
            return masked and j * LANES > (c + 1) * ATTN_ROWS - 1

        def n_tiles(c):
            return key_limit(c * ATTN_ROWS // half, masked) // LANES

        for c in range(n_chunks):
            rows = slice(c * ATTN_ROWS, (c + 1) * ATTN_ROWS)
            tmax = None
            for j in range(n_tiles(c)):
                if hidden(c, j):
                    continue
                cols = slice(j * LANES, (j + 1) * LANES)
                s = s_scr[hd, rows, cols]
                if masked and (j + 1) * LANES - 1 > c * ATTN_ROWS:
                    row = c * ATTN_ROWS + lax.broadcasted_iota(jnp.int32, (ATTN_ROWS, LANES), 0)
                    col = j * LANES + lax.broadcasted_iota(jnp.int32, (ATTN_ROWS, LANES), 1)
                    s = jnp.where(col <= row, s, NEG_INF)
                    s_scr[hd, rows, cols] = s
                tmax = s if tmax is None else jnp.maximum(tmax, s)
            m_prev = m_scr[hd, rows, :]
            m_new = jnp.maximum(m_prev, jnp.max(tmax, axis=-1, keepdims=True))
            a_scr[hd, rows, :] = jnp.exp2(m_prev - m_new)
            m_scr[hd, rows, :] = m_new
        for c in range(n_chunks):
            rows = slice(c * ATTN_ROWS, (c + 1) * ATTN_ROWS)
            m_new = m_scr[hd, rows, :]
            for j in range(n_tiles(c)):
                cols = slice(j * LANES, (j + 1) * LANES)
                if hidden(c, j):
                    p_scr[hd, rows, cols] = jnp.zeros((ATTN_ROWS, LANES), BF16)
                else:
                    p_scr[hd, rows, cols] = jnp.exp2(s_scr[hd, rows, cols] - m_new).astype(BF16)

    def accumulate(hd, masked):
        v_ones = jnp.concatenate([v_ref[hd], jnp.ones((tk, dv), BF16)], axis=1)
        for hf in range(2):
            hrows = slice(hf * half, (hf + 1) * half)
            klim = key_limit(hf, masked)
            pv = jnp.dot(p_scr[hd, hrows, 0:klim], v_ones[0:klim, :], preferred_element_type=F32)
            alpha = a_scr[hd, hrows, :]
            acc_scr[hd, hrows, :] = (jnp.concatenate([alpha] * (2 * dv // LANES), axis=1) * acc_scr[hd, hrows, :]
                                     + pv)

    def block(masked):
        for hd in range(n_hd):
            scores(hd, masked)
        for hd in range(n_hd):
            softmax(hd, masked)
            accumulate(hd, masked)

    pl.when(ki < qi)(functools.partial(block, False))

    @pl.when(ki == qi)
    def _():
        block(True)
        for hd in range(n_hd):
            acc = acc_scr[hd]
            o_ref[:, hd * dv:(hd + 1) * dv] = (acc[:, :dv] / acc[:, dv:]).astype(o_ref.dtype)


def _attention(q, k, v):
    bsz, nh, seq, dk = q.shape
    dv = v.shape[-1]
    tq = tk = min(1024, seq)
    nq = seq // tq
    pairs = [(i, j) for i in range(nq) for j in range(i + 1)]
    qi_tab = jnp.asarray([p[0] for p in pairs], jnp.int32)
    ki_tab = jnp.asarray([p[1] for p in pairs], jnp.int32)
    hp = 2 if nh % 2 == 0 else 1
    grid_spec = pltpu.PrefetchScalarGridSpec(
        num_scalar_prefetch=2,
        grid=(bsz, nh // hp, len(pairs)),
        in_specs=[pl.BlockSpec((None, hp, tq, dk), lambda b, h, s, qt, kt: (b, h, qt[s], 0)),
                  pl.BlockSpec((None, hp, tk, dk), lambda b, h, s, qt, kt: (b, h, kt[s], 0)),
                  pl.BlockSpec((None, hp, tk, dv), lambda b, h, s, qt, kt: (b, h, kt[s], 0))],
        out_specs=pl.BlockSpec((None, tq, hp * dv), lambda b, h, s, qt, kt: (b, qt[s], h)),
        scratch_shapes=[pltpu.VMEM((hp, tq, LANES), F32), pltpu.VMEM((hp, tq, LANES), F32),
                        pltpu.VMEM((hp, tq, 2 * dv), F32), pltpu.VMEM((hp, tq, tk), F32),
                        pltpu.VMEM((hp, tq, tk), BF16)],
    )
    return pl.pallas_call(
        _attn_kernel,
        grid_spec=grid_spec,
        out_shape=jax.ShapeDtypeStruct((bsz, seq, nh * dv), BF16),
        compiler_params=_params(("parallel", "parallel", "arbitrary")),
        name="attn",
    )(qi_tab, ki_tab, q, k, v)


def _mix_kernel(hr_ref, o_ref, grnn_ref, gmla_ref, x_ref, gt_ref, w1_ref, w2_ref, wo_ref, lg_ref, lb_ref,
                out_ref, *, alpha):
    y_rnn = jnp.dot(hr_ref[...], w1_ref[...], preferred_element_type=F32)
    y_mla = jnp.dot(o_ref[...], w2_ref[...], preferred_element_type=F32)
    mixed = jax.nn.sigmoid(grnn_ref[...]) * y_rnn + jax.nn.sigmoid(gmla_ref[...]) * y_mla
    z = jnp.dot(mixed.astype(BF16), wo_ref[...], preferred_element_type=F32)
    z = alpha * x_ref[...] + (1.0 + gt_ref[0]) * z
    out_ref[...] = _layer_norm(z, lg_ref[...], lb_ref[...])


def _mix(hr, o, proj, col_grnn, col_gmla, x2, gt, w1, w2, wo, lg, lb, seq, alpha):
    t, d = x2.shape
    tm = min(256, seq)
    per_b = seq // tm
    rows = lambda width, cb=0: pl.BlockSpec((tm, width), lambda i: (i, cb))
    wspec = lambda w: _resident(w.shape, lambda i: (0, 0))
    return pl.pallas_call(
        functools.partial(_mix_kernel, alpha=alpha),
        grid=(t // tm,),
        in_specs=[rows(hr.shape[1]), rows(o.shape[1]),
                  rows(d, col_grnn // d), rows(d, col_gmla // d), rows(d),
                  pl.BlockSpec((1, 1, d), lambda i: (i // per_b, 0, 0)),
                  wspec(w1), wspec(w2), wspec(wo),
                  _resident((1, d), lambda i: (0, 0)), _resident((1, d), lambda i: (0, 0))],
        out_specs=rows(d),
        out_shape=jax.ShapeDtypeStruct((t, d), F32),
        compiler_params=_params(("parallel",)),
        name="mix",
    )(hr, o, proj, proj, x2, gt, w1, w2, wo, lg, lb)


def _topk_rows(s, k):
    n, tm = s.shape
    iota = lax.broadcasted_iota(jnp.int32, (n, tm), 0).astype(F32)
    krow = lax.broadcasted_iota(jnp.int32, (k, tm), 0)
    ts = jnp.zeros((k, tm), F32)
    ti = jnp.zeros((k, tm), F32)
    for kk in range(k):
        m = jnp.max(s, axis=0, keepdims=True)
        idx = jnp.min(jnp.where(s == m, iota, float(n)), axis=0, keepdims=True)
        ts = jnp.where(krow == kk, m, ts)
        ti = jnp.where(krow == kk, idx, ti)
        s = jnp.where(iota == idx, NEG_INF, s)
    return ts, ti


def _pair_candidates(l1, l2, combine):
    assert l1.shape[0] == 16 and SUBLANES == 8
    row = lax.broadcasted_iota(jnp.int32, (SUBLANES, l1.shape[1]), 0)
    r1 = lambda i: l1[i:i + 1, :]
    lo2 = l2[0:8, :]
    sh2 = lambda d: pltpu.roll(lo2, d, axis=0)
    vregs = [
        (combine(r1(0), lo2), [(0, j) for j in range(8)]),
        (combine(r1(0), l2[8:16, :]), [(0, j) for j in range(8, 16)]),
        (combine(r1(1), lo2), [(1, j) for j in range(8)]),
        (combine(l1[8:16, :], l2[0:1, :]), [(i, 0) for i in range(8, 16)]),
        (combine(jnp.where(row < 5, r1(2), r1(4)), jnp.where(row < 5, lo2, sh2(5))),
         [(2, j) for j in range(5)] + [(4, j) for j in range(3)]),
        (combine(jnp.where(row < 4, r1(3), jnp.where(row < 6, r1(5), r1(6))),
                 jnp.where(row < 4, lo2, jnp.where(row < 6, sh2(4), sh2(6)))),
         [(3, j) for j in range(4)] + [(5, j) for j in range(2)] + [(6, j) for j in range(2)]),
        (combine(r1(7), lo2), [(7, 0), (7, 1)] + [None] * 6),
    ]
    return vregs


def _pair_topk(ts1, ti1, ts2, ti2, n_keys):
    k, tm = ts1.shape
    row = lax.broadcasted_iota(jnp.int32, (SUBLANES, tm), 0)
    rowk = lax.broadcasted_iota(jnp.int32, (k, tm), 0)
    sums = _pair_candidates(ts1, ts2, lambda a, b: a + b)
    experts = _pair_candidates(ti1, ti2, lambda a, b: a * float(n_keys) + b)
    cs, fs, es = [], [], []
    for (c, table), (e, _) in zip(sums, experts):
        flat = jnp.full((SUBLANES, tm), -1.0, F32)
        for r, ij in enumerate(table):
            if ij is None:
                c = jnp.where(row == r, NEG_INF, c)
            else:
                flat = jnp.where(row == r, float(ij[0] * k + ij[1]), flat)
        cs.append(c)
        fs.append(flat)
        es.append(e)
    cand = jnp.concatenate(cs, axis=0)
    flat = jnp.concatenate(fs, axis=0)
    expert = jnp.concatenate(es, axis=0)
    best_s = jnp.zeros((k, tm), F32)
    best_e = jnp.zeros((k, tm), F32)
    for kk in range(k):
        m = jnp.max(cand, axis=0, keepdims=True)
        fi = jnp.min(jnp.where(cand == m, flat, float(k * k)), axis=0, keepdims=True)
        sel = flat == fi
        ev = jnp.sum(jnp.where(sel, expert, 0.0), axis=0, keepdims=True)
        best_s = jnp.where(rowk == kk, m, best_s)
        best_e = jnp.where(rowk == kk, ev, best_e)
        cand = jnp.where(sel, NEG_INF, cand)
    return best_s, best_e


def _route_kernel(x_ref, sc_ref, sh_ref, wq_ref, keys_ref, h2_ref, e_ref, g_ref, q_scr, e_scr, g_scr,
                  *, n_heads):
    n_keys = keys_ref.shape[1]
    dk = keys_ref.shape[2]
    h2 = (x_ref[...] * (1.0 + sc_ref[0]) + sh_ref[0]).astype(BF16)
    h2_ref[...] = h2
    q = jnp.dot(h2, wq_ref[...], preferred_element_type=F32)
    for c in range(2 * n_heads):
        q_scr[c] = q[:, c * dk:(c + 1) * dk].astype(BF16)

    def head_body(h, carry):
        tops = []
        for p in range(2):
            s = lax.dot_general(keys_ref[p], q_scr[2 * h + p], (((1,), (1,)), ((), ())),
                                preferred_element_type=F32)
            tops.extend(_topk_rows(s, PEER_TOPK))
        best_s, best_e = _pair_topk(*tops, n_keys)
        ex = jnp.exp(best_s - jnp.max(best_s, axis=0, keepdims=True))
        gate = ex / jnp.sum(ex, axis=0, keepdims=True)
        r0 = pl.multiple_of(h * PEER_TOPK, PEER_TOPK)
        e_scr[pl.ds(r0, PEER_TOPK), :] = best_e
        g_scr[pl.ds(r0, PEER_TOPK), :] = gate
        return carry

    lax.fori_loop(0, n_heads, head_body, 0)
    e_ref[...] = e_scr[...].T
    g_ref[...] = g_scr[...].T


def _route(x1, sc, sh, wq, keys, seq):
    t, d = x1.shape
    nq = wq.shape[1]
    dk = keys.shape[2]
    n_heads = nq // (2 * dk)
    slots = n_heads * PEER_TOPK
    tm = min(256, seq)
    per_b = seq // tm
    return pl.pallas_call(
        functools.partial(_route_kernel, n_heads=n_heads),
        grid=(t // tm,),
        in_specs=[pl.BlockSpec((tm, d), lambda i: (i, 0)),
                  pl.BlockSpec((1, 1, d), lambda i: (i // per_b, 0, 0)),
                  pl.BlockSpec((1, 1, d), lambda i: (i // per_b, 0, 0)),
                  _resident(wq.shape, lambda i: (0, 0)),
                  _resident(keys.shape, lambda i: (0, 0, 0))],
        out_specs=[pl.BlockSpec((tm, d), lambda i: (i, 0)),
                   pl.BlockSpec((tm, slots), lambda i: (i, 0)),
                   pl.BlockSpec((tm, slots), lambda i: (i, 0))],
        out_shape=[jax.ShapeDtypeStruct((t, d), BF16),
                   jax.ShapeDtypeStruct((t, slots), F32),
                   jax.ShapeDtypeStruct((t, slots), F32)],
        scratch_shapes=[pltpu.VMEM((2 * n_heads, tm, dk), BF16),
                        pltpu.VMEM((slots, tm), F32),
                        pltpu.VMEM((slots, tm), F32)],
        compiler_params=_params(("parallel",)),
        name="route",
    )(x1, sc, sh, wq, keys)


GB_TOKENS = 128
GB_PITCH = GB_TOKENS + SUBLANES
GB_UNROLL = 32


def _gbuild_kernel(e_ref, g_ref, o_ref, *, n_keys):
    tmb, slots = e_ref.shape
    sub = lax.broadcasted_iota(jnp.int32, (n_keys, slots), 0).astype(F32)
    zeros = jnp.zeros((n_keys, n_keys), F32)
    for r in range(tmb, GB_PITCH):
        o_ref[pl.ds(r, n_keys, stride=GB_PITCH), :] = zeros

    def one_hots(t):
        e = e_ref[pl.ds(t, 1), :]
        g = g_ref[pl.ds(t, 1), :]
        a = jnp.floor(e * (1.0 / n_keys))
        b = e - a * float(n_keys)
        p1 = jnp.where(sub == a, g, 0.0).astype(BF16)
        p2 = jnp.where(sub == b, 1.0, 0.0).astype(BF16)
        return p1, p2

    zpad = jnp.zeros((n_keys, slots), BF16)

    def body(tb, carry):
        for u in range(0, GB_UNROLL, 2):
            ta = tb * GB_UNROLL + u
            p1a, p2a = one_hots(ta)
            p1b, p2b = one_hots(ta + 1)
            lhs = jnp.concatenate([p1a, p1b], axis=1)
            rhs = jnp.concatenate([jnp.concatenate([p2a, zpad], axis=1),
                                   jnp.concatenate([zpad, p2b], axis=1)], axis=0)
            grids = lax.dot_general(lhs, rhs, (((1,), (1,)), ((), ())), preferred_element_type=F32)
            o_ref[pl.ds(ta, n_keys, stride=GB_PITCH), :] = grids[:, 0:n_keys]
            o_ref[pl.ds(ta + 1, n_keys, stride=GB_PITCH), :] = grids[:, n_keys:]
        return carry

    lax.fori_loop(0, tmb // GB_UNROLL, body, 0)


def _gbuild(e, g, n_keys):
    t, slots = e.shape
    tmb = GB_TOKENS
    return pl.pallas_call(
        functools.partial(_gbuild_kernel, n_keys=n_keys),
        grid=(t // tmb,),
        in_specs=[pl.BlockSpec((tmb, slots), lambda i: (i, 0))] * 2,
        out_specs=pl.BlockSpec((None, n_keys * GB_PITCH, n_keys), lambda i: (i, 0, 0)),
        out_shape=jax.ShapeDtypeStruct((t // tmb, n_keys * GB_PITCH, n_keys), F32),
        compiler_params=_params(("parallel",)),
        name="gbuild",
    )(e, g)


def _peer_kernel(h2_ref, ut_ref, v_ref, g_ref, x1_ref, gt_ref, lg_ref, lb_ref, o_ref, coef_scr, *, alpha):
    j = pl.program_id(1)
    n_tiles = pl.num_programs(1) - 1
    _, n1, _, n_keys = g_ref.shape
    tm = h2_ref.shape[0]

    def score_tile():
        a = jnp.dot(h2_ref[...], ut_ref[...], preferred_element_type=F32)
        slot = j % 2
        for k in range(n1):
            cols = slice(k * n_keys, (k + 1) * n_keys)
            gate = g_ref[:, k, 0:GB_TOKENS, :].reshape(tm, n_keys)
            coef_scr[slot, :, cols] = (_gelu_tanh(a[:, cols]) * gate).astype(BF16)

    def accumulate():
        o_ref[...] += jnp.dot(coef_scr[(j + 1) % 2], v_ref[...], preferred_element_type=F32)

    @pl.when(j == 0)
    def _():
        o_ref[...] = jnp.zeros_like(o_ref)
        score_tile()

    @pl.when(jnp.logical_and(j > 0, j < n_tiles))
    def _():
        accumulate()
        score_tile()

    @pl.when(j == n_tiles)
    def _():
        accumulate()
        z = alpha * x1_ref[...] + (1.0 + gt_ref[0]) * o_ref[...]
        o_ref[...] = _layer_norm(z, lg_ref[...], lb_ref[...])


def _peer(h2, ut, v, g4, x1, gt, lg, lb, seq, alpha):
    t, d = h2.shape
    ne = v.shape[0]
    n_keys = g4.shape[3]
    tm = min(512, seq)
    te = 1024
    n1 = te // n_keys
    n_tiles = ne // te
    per_b = seq // tm
    cur = lambda j: jnp.minimum(j, n_tiles - 1)
    prev = lambda j: jnp.maximum(j - 1, 0)
    return pl.pallas_call(
        functools.partial(_peer_kernel, alpha=alpha),
        grid=(t // tm, n_tiles + 1),
        in_specs=[pl.BlockSpec((tm, d), lambda i, j: (i, 0)),
                  pl.BlockSpec((d, te), lambda i, j: (0, cur(j))),
                  pl.BlockSpec((te, d), lambda i, j: (prev(j), 0)),
                  pl.BlockSpec((tm // GB_TOKENS, n1, GB_PITCH, n_keys), lambda i, j: (i, cur(j), 0, 0)),
                  pl.BlockSpec((tm, d), lambda i, j: (i, 0)),
                  pl.BlockSpec((1, 1, d), lambda i, j: (i // per_b, 0, 0)),
                  pl.BlockSpec((1, d), lambda i, j: (0, 0)),
                  pl.BlockSpec((1, d), lambda i, j: (0, 0))],
        out_specs=pl.BlockSpec((tm, d), lambda i, j: (i, 0)),
        out_shape=jax.ShapeDtypeStruct((t, d), F32),
        scratch_shapes=[pltpu.VMEM((2, tm, te), BF16)],
        compiler_params=_params(("parallel", "arbitrary")),
        name="peer",
    )(h2, ut, v, g4, x1, gt, lg, lb)


def _swap_halves(w):
    half = w.shape[-1] // 2
    return jnp.concatenate([w[..., half:], w[..., :half]], axis=-1)


def _layer(x2, mod, cc, ss, bsz, seq, depth, p):
    t, d = x2.shape
    sh1, sc1, gt1, sh2, sc2, gt2 = [m[:, None, :] for m in jnp.split(mod, 6, axis=-1)]
    alpha = (2.0 * depth) ** 0.25
    d_rnn = p["conv_w"].shape[1]
    q_lora = p["q_norm_g"].shape[0]
    kv_lora = p["kv_norm_g"].shape[0]
    n_heads = p["w_uq"].shape[1] // (QK_NOPE + QK_ROPE)

    w_in = p["w_in"]
    widths = (d_rnn, d_rnn, q_lora, kv_lora, QK_ROPE, d, d)
    offs = [0]
    for wd in widths:
        offs.append(offs[-1] + wd)
    pieces = [w_in[:, offs[i]:offs[i + 1]] for i in range(7)]
    w_xr, w_gr, w_qc, w_kvc, w_kr, w_grnn, w_gmla = pieces
    kr_pad = 512
    w_kr_ext = jnp.concatenate([w_kr, _swap_halves(w_kr),
                                jnp.zeros((d, kr_pad - 2 * QK_ROPE), w_in.dtype)], axis=1)
    w_in_p = jnp.concatenate([w_xr, w_gr, w_grnn, w_gmla, w_qc, w_kvc, w_kr_ext], axis=1).astype(BF16)
    col_xr, col_gr, col_grnn, col_gmla = 0, d_rnn, 2 * d_rnn, 2 * d_rnn + d
    col_qc = 2 * d_rnn + 2 * d
    col_kvc = col_qc + q_lora
    col_kr = col_kvc + kv_lora

    proj = _proj(x2, sc1, sh1, w_in_p, seq)
    proj3 = proj.reshape(bsz, seq, proj.shape[1])

    row = lambda v: v[None, :]
    hr = _rglru(proj3, col_xr, col_gr, p["conv_w"], row(p["conv_b"]),
                p["w_rg_a"].astype(BF16), row(p["b_rg_a"]), p["w_rg_x"].astype(BF16), row(p["b_rg_x"]),
                row(p["rg_lambda"]))
    hr = hr.reshape(t, d_rnn)

    w_uq = p["w_uq"].reshape(q_lora, n_heads, QK_NOPE + QK_ROPE)
    w_rope = w_uq[..., QK_NOPE:]
    w_uq_ext = jnp.concatenate([w_uq[..., :QK_NOPE], w_rope, _swap_halves(w_rope)], axis=-1)
    w_uq_ext = w_uq_ext.transpose(1, 0, 2).astype(BF16)
    w_ukv = p["w_ukv"].reshape(kv_lora, n_heads, QK_NOPE + V_HEAD).transpose(1, 0, 2).astype(BF16)
    q = _qproj(proj, col_qc, row(p["q_norm_g"]), w_uq_ext, cc, ss, bsz, seq)
    k, v = _kvproj(proj, col_kvc, col_kr, row(p["kv_norm_g"]), w_ukv, cc, ss, bsz, seq)
    o = _attention(q, k, v)
    o = o.reshape(t, n_heads * V_HEAD)

    x1 = _mix(hr, o, proj, col_grnn, col_gmla, x2, gt1,
              p["w_rnn_out"].astype(BF16), p["w_mla_out"].astype(BF16), p["w_o"].astype(BF16),
              row(p["ln1_g"]), row(p["ln1_b"]), seq, alpha)

    keys = p["peer_keys"]
    n_keys = keys.shape[1]
    h2, e, g = _route(x1, sc2, sh2, p["peer_wq"].astype(BF16), keys.astype(BF16), seq)
    g3 = _gbuild(e, g, n_keys)
    g4 = g3.reshape(g3.shape[0], n_keys, GB_PITCH, n_keys)
    return _peer(h2, p["peer_u"].astype(BF16).T, p["peer_v"].astype(BF16), g4, x1, gt2,
                 row(p["ln2_g"]), row(p["ln2_b"]), seq, alpha)


def kernel(x, c, positions, w_ada, b_ada, w_in, conv_w, conv_b, w_rg_a, b_rg_a, w_rg_x, b_rg_x, rg_lambda,
           w_rnn_out, q_norm_g, w_uq, kv_norm_g, w_ukv, w_mla_out, w_o, ln1_g, ln1_b, peer_wq, peer_keys,
           peer_u, peer_v, ln2_g, ln2_b):
    bsz, seq, d = x.shape
    depth = w_ada.shape[0]
    t = bsz * seq
    stacked = dict(w_in=w_in, conv_w=conv_w, conv_b=conv_b, w_rg_a=w_rg_a, b_rg_a=b_rg_a, w_rg_x=w_rg_x,
                   b_rg_x=b_rg_x, rg_lambda=rg_lambda, w_rnn_out=w_rnn_out, q_norm_g=q_norm_g, w_uq=w_uq,
                   kv_norm_g=kv_norm_g, w_ukv=w_ukv, w_mla_out=w_mla_out, w_o=w_o, ln1_g=ln1_g, ln1_b=ln1_b,
                   peer_wq=peer_wq, peer_keys=peer_keys, peer_u=peer_u, peer_v=peer_v, ln2_g=ln2_g,
                   ln2_b=ln2_b)

    half = QK_ROPE // 2
    inv_freq = ROPE_THETA ** (-jnp.arange(0, QK_ROPE, 2, dtype=F32) / QK_ROPE)
    zeros = jnp.zeros((half,), F32)
    ones = jnp.ones((half,), F32)
    invf = jnp.concatenate([inv_freq, inv_freq, zeros, zeros])[None, :]
    cmask = jnp.concatenate([ones, ones, zeros, zeros])[None, :]
    smask = jnp.concatenate([-ones, ones, zeros, zeros])[None, :]
    cc, ss = _rope_tables(positions.reshape(t, 1), invf, cmask, smask)

    c_pad = jnp.concatenate([c, jnp.zeros((SUBLANES - bsz % SUBLANES, d), c.dtype)], axis=0)
    x2 = x.reshape(t, d)
    for l in range(depth):
        mod = _ada(c_pad, w_ada[l], b_ada[l][None, :])[:bsz]
        x2 = _layer(x2, mod, cc, ss, bsz, seq, depth, {k: v[l] for k, v in stacked.items()})
    return x2.reshape(bsz, seq, d)
```

```python
import functools
import math

import jax
import jax.numpy as jnp
from jax import lax
from jax.experimental import pallas as pl
from jax.experimental.pallas import tpu as pltpu

F32 = jnp.float32
BF16 = jnp.bfloat16

CONV_WIDTH = 4
LRU_C = 8.0
QK_NOPE = 128
QK_ROPE = 64
V_HEAD = 128
ROPE_THETA = 10000.0
PEER_TOPK = 16
LN_EPS = 1e-5
RMS_EPS = 1e-6
LANES = 128
SUBLANES = 8
VMEM_LIMIT = 56 * 1024 * 1024
NEG_INF = float("-inf")


def _params(sem):
    return pltpu.CompilerParams(dimension_semantics=sem, vmem_limit_bytes=VMEM_LIMIT)


def _resident(shape, index_map):
    return pl.BlockSpec(shape, index_map, pipeline_mode=pl.Buffered(1))


def _gelu_tanh(x):
    return 0.5 * x * (1.0 + jnp.tanh(math.sqrt(2.0 / math.pi) * (x + 0.044715 * (x * x * x))))


def _layer_norm(z, g, b):
    mu = jnp.mean(z, axis=-1, keepdims=True)
    d = z - mu
    var = jnp.mean(d * d, axis=-1, keepdims=True)
    return d * lax.rsqrt(var + LN_EPS) * g + b


def _ada_kernel(c_ref, w_ref, b_ref, o_ref):
    c = c_ref[...]
    ca = c * jax.nn.sigmoid(c)
    o_ref[...] = jnp.dot(ca, w_ref[...], preferred_element_type=F32,
                         precision=lax.Precision.HIGHEST) + b_ref[...]


def _ada(c_pad, w, b):
    m, d = c_pad.shape
    n = w.shape[1]
    tn = min(n, 1024)
    return pl.pallas_call(
        _ada_kernel,
        grid=(n // tn,),
        in_specs=[pl.BlockSpec((m, d), lambda j: (0, 0)),
                  pl.BlockSpec((d, tn), lambda j: (0, j)),
                  pl.BlockSpec((1, tn), lambda j: (0, j))],
        out_specs=pl.BlockSpec((m, tn), lambda j: (0, j)),
        out_shape=jax.ShapeDtypeStruct((m, n), F32),
        compiler_params=_params(("arbitrary",)),
        name="ada",
    )(c_pad, w, b)


def _proj_kernel(x_ref, sc_ref, sh_ref, w_ref, o_ref, a_scr):
    @pl.when(pl.program_id(1) == 0)
    def _():
        a_scr[...] = (x_ref[...] * (1.0 + sc_ref[0]) + sh_ref[0]).astype(BF16)

    o_ref[...] = jnp.dot(a_scr[...], w_ref[...], preferred_element_type=F32)


def _proj(x2, sc, sh, w, seq):
    t, d = x2.shape
    n = w.shape[1]
    tm = min(1024, seq)
    tn = 512
    per_b = seq // tm
    return pl.pallas_call(
        _proj_kernel,
        grid=(t // tm, n // tn),
        in_specs=[pl.BlockSpec((tm, d), lambda i, j: (i, 0)),
                  pl.BlockSpec((1, 1, d), lambda i, j: (i // per_b, 0, 0)),
                  pl.BlockSpec((1, 1, d), lambda i, j: (i // per_b, 0, 0)),
                  pl.BlockSpec((d, tn), lambda i, j: (0, j))],
        out_specs=pl.BlockSpec((tm, tn), lambda i, j: (i, j)),
        out_shape=jax.ShapeDtypeStruct((t, n), F32),
        scratch_shapes=[pltpu.VMEM((tm, d), BF16)],
        compiler_params=_params(("parallel", "arbitrary")),
        name="proj",
    )(x2, sc, sh, w)


def _rglru_kernel(xr_ref, gr_ref, cw_ref, cb_ref, wa_ref, ba_ref, wx_ref, bx_ref, lam_ref,
                  o_ref, xbuf, abuf, bbuf, h_scr):
    ts, tc = xr_ref.shape
    pad = SUBLANES

    @pl.when(pl.program_id(2) == 0)
    def _():
        xbuf[0:pad, :] = jnp.zeros((pad, tc), F32)
        h_scr[...] = jnp.zeros_like(h_scr)

    xbuf[pad:pad + ts, :] = xr_ref[...]
    xc = cb_ref[...] + cw_ref[0:1, :] * xbuf[pad - 3:pad - 3 + ts, :]
    for k in range(1, CONV_WIDTH):
        off = pad - (CONV_WIDTH - 1) + k
        xc = xc + cw_ref[k:k + 1, :] * xbuf[off:off + ts, :]
    xbuf[0:pad, :] = xbuf[ts:ts + pad, :]

    xcb = xc.astype(BF16)
    ra, ia = [], []
    for hh in range(tc // LANES):
        xh = xcb[:, hh * LANES:(hh + 1) * LANES]
        ra.append(jnp.dot(xh, wa_ref[hh], preferred_element_type=F32))
        ia.append(jnp.dot(xh, wx_ref[hh], preferred_element_type=F32))
    r = jax.nn.sigmoid(jnp.concatenate(ra, axis=1) + ba_ref[...])
    ig = jax.nn.sigmoid(jnp.concatenate(ia, axis=1) + bx_ref[...])
    nlam = -lam_ref[...]
    softplus = jnp.maximum(nlam, 0.0) + jnp.log1p(jnp.exp(-jnp.abs(nlam)))
    log_a = (-LRU_C * r) * softplus
    a = jnp.exp(log_a)
    b = jnp.sqrt(-jnp.tanh(log_a) * (a * a + 1.0)) * (ig * xc)

    abuf[0:ts, :] = jnp.ones((ts, tc), F32)
    bbuf[0:ts, :] = jnp.zeros((ts, tc), F32)
    d = 1
    while d < ts:
        abuf[ts:2 * ts, :] = a
        bbuf[ts:2 * ts, :] = b
        a_sh = abuf[ts - d:2 * ts - d, :]
        b_sh = bbuf[ts - d:2 * ts - d, :]
        b = a * b_sh + b
        a = a * a_sh
        d *= 2
    h = a * h_scr[...] + b
    h_scr[...] = h[ts - 1:ts, :]
    o_ref[...] = (h * _gelu_tanh(gr_ref[...])).astype(o_ref.dtype)


def _rglru(proj3, col_xr, col_gr, conv_w, conv_b, wa, ba, wx, bx, lam):
    bsz, seq, _ = proj3.shape
    d_rnn = conv_w.shape[1]
    tc = min(512, d_rnn)
    ts = min(512, seq)
    nc = d_rnn // tc
    hpt = tc // LANES
    vec = lambda: pl.BlockSpec((1, tc), lambda b, c, s: (0, c))
    return pl.pallas_call(
        _rglru_kernel,
        grid=(bsz, nc, seq // ts),
        in_specs=[pl.BlockSpec((None, ts, tc), lambda b, c, s: (b, s, col_xr // tc + c)),
                  pl.BlockSpec((None, ts, tc), lambda b, c, s: (b, s, col_gr // tc + c)),
                  pl.BlockSpec((CONV_WIDTH, tc), lambda b, c, s: (0, c)),
                  vec(),
                  pl.BlockSpec((hpt, LANES, LANES), lambda b, c, s: (c, 0, 0)),
                  vec(),
                  pl.BlockSpec((hpt, LANES, LANES), lambda b, c, s: (c, 0, 0)),
                  vec(), vec()],
        out_specs=pl.BlockSpec((None, ts, tc), lambda b, c, s: (b, s, c)),
        out_shape=jax.ShapeDtypeStruct((bsz, seq, d_rnn), BF16),
        scratch_shapes=[pltpu.VMEM((ts + 2 * SUBLANES, tc), F32),
                        pltpu.VMEM((2 * ts, tc), F32),
                        pltpu.VMEM((2 * ts, tc), F32),
                        pltpu.VMEM((1, tc), F32)],
        compiler_params=_params(("parallel", "parallel", "arbitrary")),
        name="rglru",
    )(proj3, proj3, conv_w, conv_b, wa, ba, wx, bx, lam)


def _rope_kernel(pos_ref, invf_ref, cm_ref, sm_ref, cc_ref, ss_ref):
    ang = pos_ref[...].astype(F32) * invf_ref[...]
    cc_ref[...] = jnp.cos(ang) * cm_ref[...]
    ss_ref[...] = jnp.sin(ang) * sm_ref[...]


def _rope_tables(pos_col, invf, cmask, smask):
    t = pos_col.shape[0]
    tm = min(1024, t)
    row = lambda: pl.BlockSpec((1, LANES), lambda i: (0, 0))
    return pl.pallas_call(
        _rope_kernel,
        grid=(t // tm,),
        in_specs=[pl.BlockSpec((tm, 1), lambda i: (i, 0)), row(), row(), row()],
        out_specs=[pl.BlockSpec((tm, LANES), lambda i: (i, 0))] * 2,
        out_shape=[jax.ShapeDtypeStruct((t, LANES), F32)] * 2,
        compiler_params=_params(("parallel",)),
        name="rope",
    )(pos_col, invf, cmask, smask)


def _rms_bf16(x, g):
    return (x * lax.rsqrt(jnp.mean(x * x, axis=-1, keepdims=True) + RMS_EPS) * g).astype(BF16)


def _rope_lanes(r, cc, ss):
    return r * cc + pltpu.roll(r, LANES // 2, axis=1) * ss


PROJ_HEADS = 4


def _qproj_kernel(qc_ref, g_ref, w_ref, cc_ref, ss_ref, o_ref, a_scr):
    @pl.when(pl.program_id(1) == 0)
    def _():
        a_scr[...] = _rms_bf16(qc_ref[...], g_ref[...])

    qscale = (QK_NOPE + QK_ROPE) ** -0.5 * math.log2(math.e)
    a = a_scr[...]
    for hd in range(w_ref.shape[0]):
        acc = jnp.dot(a, w_ref[hd], preferred_element_type=F32)
        o_ref[hd, :, 0:QK_NOPE] = (acc[:, 0:QK_NOPE] * qscale).astype(o_ref.dtype)
        roped = _rope_lanes(acc[:, QK_NOPE:], cc_ref[...], ss_ref[...])
        o_ref[hd, :, QK_NOPE:] = (roped * qscale).astype(o_ref.dtype)


def _heads_per_step(nh):
    return PROJ_HEADS if nh % PROJ_HEADS == 0 else 1


def _qproj(proj, col_qc, g, w_ext, cc, ss, bsz, seq):
    t = proj.shape[0]
    nh, k, n = w_ext.shape
    hp = _heads_per_step(nh)
    tm = min(1024, seq)
    per_b = seq // tm
    return pl.pallas_call(
        _qproj_kernel,
        grid=(t // tm, nh // hp),
        in_specs=[pl.BlockSpec((tm, k), lambda i, h: (i, col_qc // k)),
                  pl.BlockSpec((1, k), lambda i, h: (0, 0)),
                  pl.BlockSpec((hp, k, n), lambda i, h: (h, 0, 0)),
                  pl.BlockSpec((tm, LANES), lambda i, h: (i, 0)),
                  pl.BlockSpec((tm, LANES), lambda i, h: (i, 0))],
        out_specs=pl.BlockSpec((None, hp, tm, n), lambda i, h: (i // per_b, h, i % per_b, 0)),
        out_shape=jax.ShapeDtypeStruct((bsz, nh, seq, n), BF16),
        scratch_shapes=[pltpu.VMEM((tm, k), BF16)],
        compiler_params=_params(("parallel", "arbitrary")),
        name="qproj",
    )(proj, g, w_ext, cc, ss)


def _kvproj_kernel(kvc_ref, kr_ref, g_ref, w_ref, cc_ref, ss_ref, k_ref, v_ref, a_scr, kr_scr):
    @pl.when(pl.program_id(1) == 0)
    def _():
        a_scr[...] = _rms_bf16(kvc_ref[...], g_ref[...])
        kr_scr[...] = _rope_lanes(kr_ref[...], cc_ref[...], ss_ref[...]).astype(kr_scr.dtype)

    a = a_scr[...]
    for hd in range(w_ref.shape[0]):
        acc = jnp.dot(a, w_ref[hd], preferred_element_type=F32)
        k_ref[hd, :, 0:QK_NOPE] = acc[:, 0:QK_NOPE].astype(k_ref.dtype)
        k_ref[hd, :, QK_NOPE:] = kr_scr[...]
        v_ref[hd] = acc[:, QK_NOPE:].astype(v_ref.dtype)


def _kvproj(proj, col_kvc, col_kr, g, w_h, cc, ss, bsz, seq):
    t = proj.shape[0]
    nh, k, n = w_h.shape
    hp = _heads_per_step(nh)
    tm = min(1024, seq)
    per_b = seq // tm
    omap = lambda i, h: (i // per_b, h, i % per_b, 0)
    return pl.pallas_call(
        _kvproj_kernel,
        grid=(t // tm, nh // hp),
        in_specs=[pl.BlockSpec((tm, k), lambda i, h: (i, col_kvc // k)),
                  pl.BlockSpec((tm, LANES), lambda i, h: (i, col_kr // LANES)),
                  pl.BlockSpec((1, k), lambda i, h: (0, 0)),
                  pl.BlockSpec((hp, k, n), lambda i, h: (h, 0, 0)),
                  pl.BlockSpec((tm, LANES), lambda i, h: (i, 0)),
                  pl.BlockSpec((tm, LANES), lambda i, h: (i, 0))],
        out_specs=[pl.BlockSpec((None, hp, tm, 2 * LANES), omap),
                   pl.BlockSpec((None, hp, tm, V_HEAD), omap)],
        out_shape=[jax.ShapeDtypeStruct((bsz, nh, seq, 2 * LANES), BF16),
                   jax.ShapeDtypeStruct((bsz, nh, seq, V_HEAD), BF16)],
        scratch_shapes=[pltpu.VMEM((tm, k), BF16), pltpu.VMEM((tm, LANES), BF16)],
        compiler_params=_params(("parallel", "arbitrary")),
        name="kvproj",
    )(proj, proj, g, w_h, cc, ss)


ATTN_ROWS = 32
ATTN_HEADS = 4


def _attn_kernel(qi_tab, ki_tab, q_ref, k_ref, v_ref, o_ref, m_scr, a_scr, acc_scr, s_scr, p_scr):
    step = pl.program_id(2)
    qi, ki = qi_tab[step], ki_tab[step]
    n_hd, tq, _ = q_ref.shape
    tk, dv = v_ref.shape[1], v_ref.shape[2]
    half = tq // 2
    n_chunks = tq // ATTN_ROWS

    @pl.when(ki == 0)
    def _():
        m_scr[...] = jnp.full_like(m_scr, NEG_INF)
        acc_scr[...] = jnp.zeros_like(acc_scr)

    def key_limit(hf, masked):
        return (hf + 1) * half if masked else tk

    def scores(hd, masked):
        for hf in range(2):
            hrows = slice(hf * half, (hf + 1) * half)
            klim = key_limit(hf, masked)
            s_scr[hd, hrows, 0:klim] = lax.dot_general(q_ref[hd, hrows, :], k_ref[hd, 0:klim, :],
                                                       (((1,), (1,)), ((), ())), preferred_element_type=F32)

    def softmax(hd, masked):
        def hidden(c, j):
            return masked and j * LANES > (c + 1) * ATTN_ROWS - 1

        def n_tiles(c):
            return key_limit(c * ATTN_ROWS // half, masked) // LANES

        for c in range(n_chunks):
            rows = slice(c * ATTN_ROWS, (c + 1) * ATTN_ROWS)
            tmax = None
            for j in range(n_tiles(c)):
                if hidden(c, j):
                    continue
                cols = slice(j * LANES, (j + 1) * LANES)
                s = s_scr[hd, rows, cols]
                if masked and (j + 1) * LANES - 1 > c * ATTN_ROWS:
                    row = c * ATTN_ROWS + lax.broadcasted_iota(jnp.int32, (ATTN_ROWS, LANES), 0)
                    col = j * LANES + lax.broadcasted_iota(jnp.int32, (ATTN_ROWS, LANES), 1)
                    s = jnp.where(col <= row, s, NEG_INF)
                    s_scr[hd, rows, cols] = s
                tmax = s if tmax is None else jnp.maximum(tmax, s)
            m_prev = m_scr[hd, rows, :]
            m_new = jnp.maximum(m_prev, jnp.max(tmax, axis=-1, keepdims=True))
            a_scr[hd, rows, :] = jnp.exp2(m_prev - m_new)
            m_scr[hd, rows, :] = m_new
        for c in range(n_chunks):
            rows = slice(c * ATTN_ROWS, (c + 1) * ATTN_ROWS)
            m_new = m_scr[hd, rows, :]
            for j in range(n_tiles(c)):
                cols = slice(j * LANES, (j + 1) * LANES)
                if hidden(c, j):
                    p_scr[hd, rows, cols] = jnp.zeros((ATTN_ROWS, LANES), BF16)
                else:
                    p_scr[hd, rows, cols] = jnp.exp2(s_scr[hd, rows, cols] - m_new).astype(BF16)

    def accumulate(hd, masked):
        v_ones = jnp.concatenate([v_ref[hd], jnp.ones((tk, dv), BF16)], axis=1)
        for hf in range(2):
            hrows = slice(hf * half, (hf + 1) * half)
            klim = key_limit(hf, masked)
            pv = jnp.dot(p_scr[hd, hrows, 0:klim], v_ones[0:klim, :], preferred_element_type=F32)
            alpha = a_scr[hd, hrows, :]
            acc_scr[hd, hrows, :] = (jnp.concatenate([alpha] * (2 * dv // LANES), axis=1) * acc_scr[hd, hrows, :]
                                     + pv)

    def block(masked):
        for hd in range(n_hd):
            scores(hd, masked)
        for hd in range(n_hd):
            softmax(hd, masked)
            accumulate(hd, masked)

    pl.when(ki < qi)(functools.partial(block, False))

    @pl.when(ki == qi)
    def _():
        block(True)
        for hd in range(n_hd):
            acc = acc_scr[hd]
            o_ref[:, hd * dv:(hd + 1) * dv] = (acc[:, :dv] / acc[:, dv:]).astype(o_ref.dtype)


def _attention(q, k, v):
    bsz, nh, seq, dk = q.shape
    dv = v.shape[-1]
    tq = tk = min(1024, seq)
    nq = seq // tq
    pairs = [(i, j) for i in range(nq) for j in range(i + 1)]
    qi_tab = jnp.asarray([p[0] for p in pairs], jnp.int32)
    ki_tab = jnp.asarray([p[1] for p in pairs], jnp.int32)
    hp = ATTN_HEADS if nh % ATTN_HEADS == 0 else 1
    grid_spec = pltpu.PrefetchScalarGridSpec(
        num_scalar_prefetch=2,
        grid=(bsz, nh // hp, len(pairs)),
        in_specs=[pl.BlockSpec((None, hp, tq, dk), lambda b, h, s, qt, kt: (b, h, qt[s], 0)),
                  pl.BlockSpec((None, hp, tk, dk), lambda b, h, s, qt, kt: (b, h, kt[s], 0)),
                  pl.BlockSpec((None, hp, tk, dv), lambda b, h, s, qt, kt: (b, h, kt[s], 0))],
        out_specs=pl.BlockSpec((None, tq, hp * dv), lambda b, h, s, qt, kt: (b, qt[s], h)),
        scratch_shapes=[pltpu.VMEM((hp, tq, LANES), F32), pltpu.VMEM((hp, tq, LANES), F32),
                        pltpu.VMEM((hp, tq, 2 * dv), F32), pltpu.VMEM((hp, tq, tk), F32),
                        pltpu.VMEM((hp, tq, tk), BF16)],
    )
    return pl.pallas_call(
        _attn_kernel,
        grid_spec=grid_spec,
        out_shape=jax.ShapeDtypeStruct((bsz, seq, nh * dv), BF16),
        compiler_params=_params(("parallel", "parallel", "arbitrary")),
        name="attn",
    )(qi_tab, ki_tab, q, k, v)


def _mix_kernel(hr_ref, o_ref, grnn_ref, gmla_ref, x_ref, gt_ref, w1_ref, w2_ref, wo_ref, lg_ref, lb_ref,
                out_ref, *, alpha):
    y_rnn = jnp.dot(hr_ref[...], w1_ref[...], preferred_element_type=F32)
    y_mla = jnp.dot(o_ref[...], w2_ref[...], preferred_element_type=F32)
    mixed = jax.nn.sigmoid(grnn_ref[...]) * y_rnn + jax.nn.sigmoid(gmla_ref[...]) * y_mla
    z = jnp.dot(mixed.astype(BF16), wo_ref[...], preferred_element_type=F32)
    z = alpha * x_ref[...] + (1.0 + gt_ref[0]) * z
    out_ref[...] = _layer_norm(z, lg_ref[...], lb_ref[...])


def _mix(hr, o, proj, col_grnn, col_gmla, x2, gt, w1, w2, wo, lg, lb, seq, alpha):
    t, d = x2.shape
    tm = min(256, seq)
    per_b = seq // tm
    rows = lambda width, cb=0: pl.BlockSpec((tm, width), lambda i: (i, cb))
    wspec = lambda w: _resident(w.shape, lambda i: (0, 0))
    return pl.pallas_call(
        functools.partial(_mix_kernel, alpha=alpha),
        grid=(t // tm,),
        in_specs=[rows(hr.shape[1]), rows(o.shape[1]),
                  rows(d, col_grnn // d), rows(d, col_gmla // d), rows(d),
                  pl.BlockSpec((1, 1, d), lambda i: (i // per_b, 0, 0)),
                  wspec(w1), wspec(w2), wspec(wo),
                  _resident((1, d), lambda i: (0, 0)), _resident((1, d), lambda i: (0, 0))],
        out_specs=rows(d),
        out_shape=jax.ShapeDtypeStruct((t, d), F32),
        compiler_params=_params(("parallel",)),
        name="mix",
    )(hr, o, proj, proj, x2, gt, w1, w2, wo, lg, lb)


def _topk_rows(s, k):
    n, tm = s.shape
    iota = lax.broadcasted_iota(jnp.int32, (n, tm), 0).astype(F32)
    krow = lax.broadcasted_iota(jnp.int32, (k, tm), 0)
    ts = jnp.zeros((k, tm), F32)
    ti = jnp.zeros((k, tm), F32)
    for kk in range(k):
        m = jnp.max(s, axis=0, keepdims=True)
        idx = jnp.min(jnp.where(s == m, iota, float(n)), axis=0, keepdims=True)
        ts = jnp.where(krow == kk, m, ts)
        ti = jnp.where(krow == kk, idx, ti)
        s = jnp.where(iota == idx, NEG_INF, s)
    return ts, ti


def _pair_candidates(l1, l2, combine):
    assert l1.shape[0] == 16 and SUBLANES == 8
    row = lax.broadcasted_iota(jnp.int32, (SUBLANES, l1.shape[1]), 0)
    r1 = lambda i: l1[i:i + 1, :]
    lo2 = l2[0:8, :]
    sh2 = lambda d: pltpu.roll(lo2, d, axis=0)
    vregs = [
        (combine(r1(0), lo2), [(0, j) for j in range(8)]),
        (combine(r1(0), l2[8:16, :]), [(0, j) for j in range(8, 16)]),
        (combine(r1(1), lo2), [(1, j) for j in range(8)]),
        (combine(l1[8:16, :], l2[0:1, :]), [(i, 0) for i in range(8, 16)]),
        (combine(jnp.where(row < 5, r1(2), r1(4)), jnp.where(row < 5, lo2, sh2(5))),
         [(2, j) for j in range(5)] + [(4, j) for j in range(3)]),
        (combine(jnp.where(row < 4, r1(3), jnp.where(row < 6, r1(5), r1(6))),
                 jnp.where(row < 4, lo2, jnp.where(row < 6, sh2(4), sh2(6)))),
         [(3, j) for j in range(4)] + [(5, j) for j in range(2)] + [(6, j) for j in range(2)]),
        (combine(r1(7), lo2), [(7, 0), (7, 1)] + [None] * 6),
    ]
    return vregs


def _pair_topk(ts1, ti1, ts2, ti2, n_keys):
    k, tm = ts1.shape
    row = lax.broadcasted_iota(jnp.int32, (SUBLANES, tm), 0)
    rowk = lax.broadcasted_iota(jnp.int32, (k, tm), 0)
    sums = _pair_candidates(ts1, ts2, lambda a, b: a + b)
    experts = _pair_candidates(ti1, ti2, lambda a, b: a * float(n_keys) + b)
    cs, fs, es = [], [], []
    for (c, table), (e, _) in zip(sums, experts):
        flat = jnp.full((SUBLANES, tm), -1.0, F32)
        for r, ij in enumerate(table):
            if ij is None:
                c = jnp.where(row == r, NEG_INF, c)
            else:
                flat = jnp.where(row == r, float(ij[0] * k + ij[1]), flat)
        cs.append(c)
        fs.append(flat)
        es.append(e)
    cand = jnp.concatenate(cs, axis=0)
    flat = jnp.concatenate(fs, axis=0)
    expert = jnp.concatenate(es, axis=0)
    best_s = jnp.zeros((k, tm), F32)
    best_e = jnp.zeros((k, tm), F32)
    for kk in range(k):
        m = jnp.max(cand, axis=0, keepdims=True)
        fi = jnp.min(jnp.where(cand == m, flat, float(k * k)), axis=0, keepdims=True)
        sel = flat == fi
        ev = jnp.sum(jnp.where(sel, expert, 0.0), axis=0, keepdims=True)
        best_s = jnp.where(rowk == kk, m, best_s)
        best_e = jnp.where(rowk == kk, ev, best_e)
        cand = jnp.where(sel, NEG_INF, cand)
    return best_s, best_e


def _route_kernel(x_ref, sc_ref, sh_ref, wq_ref, keys_ref, h2_ref, e_ref, g_ref, q_scr, e_scr, g_scr,
                  *, n_heads):
    n_keys = keys_ref.shape[1]
    dk = keys_ref.shape[2]
    h2 = (x_ref[...] * (1.0 + sc_ref[0]) + sh_ref[0]).astype(BF16)
    h2_ref[...] = h2
    q = jnp.dot(h2, wq_ref[...], preferred_element_type=F32)
    for c in range(2 * n_heads):
        q_scr[c] = q[:, c * dk:(c + 1) * dk].astype(BF16)

    def head_body(h, carry):
        tops = []
        for p in range(2):
            s = lax.dot_general(keys_ref[p], q_scr[2 * h + p], (((1,), (1,)), ((), ())),
                                preferred_element_type=F32)
            tops.extend(_topk_rows(s, PEER_TOPK))
        best_s, best_e = _pair_topk(*tops, n_keys)
        ex = jnp.exp(best_s - jnp.max(best_s, axis=0, keepdims=True))
        gate = ex / jnp.sum(ex, axis=0, keepdims=True)
        r0 = pl.multiple_of(h * PEER_TOPK, PEER_TOPK)
        e_scr[pl.ds(r0, PEER_TOPK), :] = best_e
        g_scr[pl.ds(r0, PEER_TOPK), :] = gate
        return carry

    lax.fori_loop(0, n_heads, head_body, 0)
    e_ref[...] = e_scr[...].T
    g_ref[...] = g_scr[...].T


def _route(x1, sc, sh, wq, keys, seq):
    t, d = x1.shape
    nq = wq.shape[1]
    dk = keys.shape[2]
    n_heads = nq // (2 * dk)
    slots = n_heads * PEER_TOPK
    tm = min(256, seq)
    per_b = seq // tm
    return pl.pallas_call(
        functools.partial(_route_kernel, n_heads=n_heads),
        grid=(t // tm,),
        in_specs=[pl.BlockSpec((tm, d), lambda i: (i, 0)),
                  pl.BlockSpec((1, 1, d), lambda i: (i // per_b, 0, 0)),
                  pl.BlockSpec((1, 1, d), lambda i: (i // per_b, 0, 0)),
                  _resident(wq.shape, lambda i: (0, 0)),
                  _resident(keys.shape, lambda i: (0, 0, 0))],
        out_specs=[pl.BlockSpec((tm, d), lambda i: (i, 0)),
                   pl.BlockSpec((tm, slots), lambda i: (i, 0)),
                   pl.BlockSpec((tm, slots), lambda i: (i, 0))],
        out_shape=[jax.ShapeDtypeStruct((t, d), BF16),
                   jax.ShapeDtypeStruct((t, slots), F32),
                   jax.ShapeDtypeStruct((t, slots), F32)],
        scratch_shapes=[pltpu.VMEM((2 * n_heads, tm, dk), BF16),
                        pltpu.VMEM((slots, tm), F32),
                        pltpu.VMEM((slots, tm), F32)],
        compiler_params=_params(("parallel",)),
        name="route",
    )(x1, sc, sh, wq, keys)


GB_TOKENS = 128
GB_PITCH = GB_TOKENS + SUBLANES
GB_UNROLL = 64


def _gbuild_kernel(e_ref, g_ref, o_ref, *, n_keys):
    tmb, slots = e_ref.shape
    sub = lax.broadcasted_iota(jnp.int32, (n_keys, slots), 0).astype(F32)
    zeros = jnp.zeros((n_keys, n_keys), F32)
    for r in range(tmb, GB_PITCH):
        o_ref[pl.ds(r, n_keys, stride=GB_PITCH), :] = zeros

    def one_hots(t):
        e = e_ref[pl.ds(t, 1), :]
        g = g_ref[pl.ds(t, 1), :]
        a = jnp.floor(e * (1.0 / n_keys))
        b = e - a * float(n_keys)
        p1 = jnp.where(sub == a, g, 0.0).astype(BF16)
        p2 = jnp.where(sub == b, 1.0, 0.0).astype(BF16)
        return p1, p2

    zpad = jnp.zeros((n_keys, slots), BF16)

    def body(tb, carry):
        for u in range(0, GB_UNROLL, 2):
            ta = tb * GB_UNROLL + u
            p1a, p2a = one_hots(ta)
            p1b, p2b = one_hots(ta + 1)
            lhs = jnp.concatenate([p1a, p1b], axis=1)
            rhs = jnp.concatenate([jnp.concatenate([p2a, zpad], axis=1),
                                   jnp.concatenate([zpad, p2b], axis=1)], axis=0)
            grids = lax.dot_general(lhs, rhs, (((1,), (1,)), ((), ())), preferred_element_type=F32)
            o_ref[pl.ds(ta, n_keys, stride=GB_PITCH), :] = grids[:, 0:n_keys]
            o_ref[pl.ds(ta + 1, n_keys, stride=GB_PITCH), :] = grids[:, n_keys:]
        return carry

    lax.fori_loop(0, tmb // GB_UNROLL, body, 0)


def _gbuild(e, g, n_keys):
    t, slots = e.shape
    tmb = GB_TOKENS
    return pl.pallas_call(
        functools.partial(_gbuild_kernel, n_keys=n_keys),
        grid=(t // tmb,),
        in_specs=[pl.BlockSpec((tmb, slots), lambda i: (i, 0))] * 2,
        out_specs=pl.BlockSpec((None, n_keys * GB_PITCH, n_keys), lambda i: (i, 0, 0)),
        out_shape=jax.ShapeDtypeStruct((t // tmb, n_keys * GB_PITCH, n_keys), F32),
        compiler_params=_params(("parallel",)),
        name="gbuild",
    )(e, g)


def _peer_kernel(h2_ref, ut_ref, v_ref, g_ref, x1_ref, gt_ref, lg_ref, lb_ref, o_ref, coef_scr, *, alpha):
    j = pl.program_id(1)
    n_tiles = pl.num_programs(1) - 1
    _, n1, _, n_keys = g_ref.shape
    tm = h2_ref.shape[0]

    def score_tile():
        a = jnp.dot(h2_ref[...], ut_ref[...], preferred_element_type=F32)
        slot = j % 2
        for k in range(n1):
            cols = slice(k * n_keys, (k + 1) * n_keys)
            gate = g_ref[:, k, 0:GB_TOKENS, :].reshape(tm, n_keys)
            coef_scr[slot, :, cols] = (_gelu_tanh(a[:, cols]) * gate).astype(BF16)

    def accumulate():
        o_ref[...] += jnp.dot(coef_scr[(j + 1) % 2], v_ref[...], preferred_element_type=F32)

    @pl.when(j == 0)
    def _():
        o_ref[...] = jnp.zeros_like(o_ref)
        score_tile()

    @pl.when(jnp.logical_and(j > 0, j < n_tiles))
    def _():
        accumulate()
        score_tile()

    @pl.when(j == n_tiles)
    def _():
        accumulate()
        z = alpha * x1_ref[...] + (1.0 + gt_ref[0]) * o_ref[...]
        o_ref[...] = _layer_norm(z, lg_ref[...], lb_ref[...])


def _peer(h2, ut, v, g4, x1, gt, lg, lb, seq, alpha):
    t, d = h2.shape
    ne = v.shape[0]
    n_keys = g4.shape[3]
    tm = min(512, seq)
    te = 1024
    n1 = te // n_keys
    n_tiles = ne // te
    per_b = seq // tm
    cur = lambda j: jnp.minimum(j, n_tiles - 1)
    prev = lambda j: jnp.maximum(j - 1, 0)
    return pl.pallas_call(
        functools.partial(_peer_kernel, alpha=alpha),
        grid=(t // tm, n_tiles + 1),
        in_specs=[pl.BlockSpec((tm, d), lambda i, j: (i, 0)),
                  pl.BlockSpec((d, te), lambda i, j: (0, cur(j))),
                  pl.BlockSpec((te, d), lambda i, j: (prev(j), 0)),
                  pl.BlockSpec((tm // GB_TOKENS, n1, GB_PITCH, n_keys), lambda i, j: (i, cur(j), 0, 0)),
                  pl.BlockSpec((tm, d), lambda i, j: (i, 0)),
                  pl.BlockSpec((1, 1, d), lambda i, j: (i // per_b, 0, 0)),
                  pl.BlockSpec((1, d), lambda i, j: (0, 0)),
                  pl.BlockSpec((1, d), lambda i, j: (0, 0))],
        out_specs=pl.BlockSpec((tm, d), lambda i, j: (i, 0)),
        out_shape=jax.ShapeDtypeStruct((t, d), F32),
        scratch_shapes=[pltpu.VMEM((2, tm, te), BF16)],
        compiler_params=_params(("parallel", "arbitrary")),
        name="peer",
    )(h2, ut, v, g4, x1, gt, lg, lb)


def _swap_halves(w):
    half = w.shape[-1] // 2
    return jnp.concatenate([w[..., half:], w[..., :half]], axis=-1)


def _layer(x2, mod, cc, ss, bsz, seq, depth, p):
    t, d = x2.shape
    sh1, sc1, gt1, sh2, sc2, gt2 = [m[:, None, :] for m in jnp.split(mod, 6, axis=-1)]
    alpha = (2.0 * depth) ** 0.25
    d_rnn = p["conv_w"].shape[1]
    q_lora = p["q_norm_g"].shape[0]
    kv_lora = p["kv_norm_g"].shape[0]
    n_heads = p["w_uq"].shape[1] // (QK_NOPE + QK_ROPE)

    w_in = p["w_in"]
    widths = (d_rnn, d_rnn, q_lora, kv_lora, QK_ROPE, d, d)
    offs = [0]
    for wd in widths:
        offs.append(offs[-1] + wd)
    pieces = [w_in[:, offs[i]:offs[i + 1]] for i in range(7)]
    w_xr, w_gr, w_qc, w_kvc, w_kr, w_grnn, w_gmla = pieces
    kr_pad = 512
    w_kr_ext = jnp.concatenate([w_kr, _swap_halves(w_kr),
                                jnp.zeros((d, kr_pad - 2 * QK_ROPE), w_in.dtype)], axis=1)
    w_in_p = jnp.concatenate([w_xr, w_gr, w_grnn, w_gmla, w_qc, w_kvc, w_kr_ext], axis=1).astype(BF16)
    col_xr, col_gr, col_grnn, col_gmla = 0, d_rnn, 2 * d_rnn, 2 * d_rnn + d
    col_qc = 2 * d_rnn + 2 * d
    col_kvc = col_qc + q_lora
    col_kr = col_kvc + kv_lora

    proj = _proj(x2, sc1, sh1, w_in_p, seq)
    proj3 = proj.reshape(bsz, seq, proj.shape[1])

    row = lambda v: v[None, :]
    hr = _rglru(proj3, col_xr, col_gr, p["conv_w"], row(p["conv_b"]),
                p["w_rg_a"].astype(BF16), row(p["b_rg_a"]), p["w_rg_x"].astype(BF16), row(p["b_rg_x"]),
                row(p["rg_lambda"]))
    hr = hr.reshape(t, d_rnn)

    w_uq = p["w_uq"].reshape(q_lora, n_heads, QK_NOPE + QK_ROPE)
    w_rope = w_uq[..., QK_NOPE:]
    w_uq_ext = jnp.concatenate([w_uq[..., :QK_NOPE], w_rope, _swap_halves(w_rope)], axis=-1)
    w_uq_ext = w_uq_ext.transpose(1, 0, 2).astype(BF16)
    w_ukv = p["w_ukv"].reshape(kv_lora, n_heads, QK_NOPE + V_HEAD).transpose(1, 0, 2).astype(BF16)
    q = _qproj(proj, col_qc, row(p["q_norm_g"]), w_uq_ext, cc, ss, bsz, seq)
    k, v = _kvproj(proj, col_kvc, col_kr, row(p["kv_norm_g"]), w_ukv, cc, ss, bsz, seq)
    o = _attention(q, k, v)
    o = o.reshape(t, n_heads * V_HEAD)

    x1 = _mix(hr, o, proj, col_grnn, col_gmla, x2, gt1,
              p["w_rnn_out"].astype(BF16), p["w_mla_out"].astype(BF16), p["w_o"].astype(BF16),
              row(p["ln1_g"]), row(p["ln1_b"]), seq, alpha)

    keys = p["peer_keys"]
    n_keys = keys.shape[1]
    h2, e, g = _route(x1, sc2, sh2, p["peer_wq"].astype(BF16), keys.astype(BF16), seq)
    g3 = _gbuild(e, g, n_keys)
    g4 = g3.reshape(g3.shape[0], n_keys, GB_PITCH, n_keys)
    return _peer(h2, p["peer_u"].astype(BF16).T, p["peer_v"].astype(BF16), g4, x1, gt2,
                 row(p["ln2_g"]), row(p["ln2_b"]), seq, alpha)


def kernel(x, c, positions, w_ada, b_ada, w_in, conv_w, conv_b, w_rg_a, b_rg_a, w_rg_x, b_rg_x, rg_lambda,
           w_rnn_out, q_norm_g, w_uq, kv_norm_g, w_ukv, w_mla_out, w_o, ln1_g, ln1_b, peer_wq, peer_keys,
           peer_u, peer_v, ln2_g, ln2_b):
    bsz, seq, d = x.shape
    depth = w_ada.shape[0]
    t = bsz * seq
    stacked = dict(w_in=w_in, conv_w=conv_w, conv_b=conv_b, w_rg_a=w_rg_a, b_rg_a=b_rg_a, w_rg_x=w_rg_x,
                   b_rg_x=b_rg_x, rg_lambda=rg_lambda, w_rnn_out=w_rnn_out, q_norm_g=q_norm_g, w_uq=w_uq,
                   kv_norm_g=kv_norm_g, w_ukv=w_ukv, w_mla_out=w_mla_out, w_o=w_o, ln1_g=ln1_g, ln1_b=ln1_b,
                   peer_wq=peer_wq, peer_keys=peer_keys, peer_u=peer_u, peer_v=peer_v, ln2_g=ln2_g,
                   ln2_b=ln2_b)

    half = QK_ROPE // 2
    inv_freq = ROPE_THETA ** (-jnp.arange(0, QK_ROPE, 2, dtype=F32) / QK_ROPE)
    zeros = jnp.zeros((half,), F32)
    ones = jnp.ones((half,), F32)
    invf = jnp.concatenate([inv_freq, inv_freq, zeros, zeros])[None, :]
    cmask = jnp.concatenate([ones, ones, zeros, zeros])[None, :]
    smask = jnp.concatenate([-ones, ones, zeros, zeros])[None, :]
    cc, ss = _rope_tables(positions.reshape(t, 1), invf, cmask, smask)

    c_pad = jnp.concatenate([c, jnp.zeros((SUBLANES - bsz % SUBLANES, d), c.dtype)], axis=0)
    x2 = x.reshape(t, d)
    for l in range(depth):
        mod = _ada(c_pad, w_ada[l], b_ada[l][None, :])[:bsz]
        x2 = _layer(x2, mod, cc, ss, bsz, seq, depth, {k: v[l] for k, v in stacked.items()})
    return x2.reshape(bsz, seq, d)
```

```python
import functools
import math

import jax
import jax.numpy as jnp
from jax import lax
from jax.experimental import pallas as pl
from jax.experimental.pallas import tpu as pltpu

F32 = jnp.float32
BF16 = jnp.bfloat16

CONV_WIDTH = 4
LRU_C = 8.0
QK_NOPE = 128
QK_ROPE = 64
V_HEAD = 128
ROPE_THETA = 10000.0
PEER_TOPK = 16
LN_EPS = 1e-5
RMS_EPS = 1e-6
LANES = 128
SUBLANES = 8
VMEM_LIMIT = 56 * 1024 * 1024
NEG_INF = float("-inf")


def _params(sem):
    return pltpu.CompilerParams(dimension_semantics=sem, vmem_limit_bytes=VMEM_LIMIT)


def _resident(shape, index_map):
    return pl.BlockSpec(shape, index_map, pipeline_mode=pl.Buffered(1))


def _gelu_tanh(x):
    return 0.5 * x * (1.0 + jnp.tanh(math.sqrt(2.0 / math.pi) * (x + 0.044715 * (x * x * x))))


def _layer_norm(z, g, b):
    mu = jnp.mean(z, axis=-1, keepdims=True)
    d = z - mu
    var = jnp.mean(d * d, axis=-1, keepdims=True)
    return d * lax.rsqrt(var + LN_EPS) * g + b


def _ada_kernel(c_ref, w_ref, b_ref, o_ref):
    c = c_ref[...]
    ca = c * jax.nn.sigmoid(c)
    o_ref[...] = jnp.dot(ca, w_ref[...], preferred_element_type=F32,
                         precision=lax.Precision.HIGHEST) + b_ref[...]


def _ada(c_pad, w, b):
    m, d = c_pad.shape
    n = w.shape[1]
    tn = min(n, 1024)
    return pl.pallas_call(
        _ada_kernel,
        grid=(n // tn,),
        in_specs=[pl.BlockSpec((m, d), lambda j: (0, 0)),
                  pl.BlockSpec((d, tn), lambda j: (0, j)),
                  pl.BlockSpec((1, tn), lambda j: (0, j))],
        out_specs=pl.BlockSpec((m, tn), lambda j: (0, j)),
        out_shape=jax.ShapeDtypeStruct((m, n), F32),
        compiler_params=_params(("arbitrary",)),
        name="ada",
    )(c_pad, w, b)


def _proj_kernel(x_ref, sc_ref, sh_ref, w_ref, o_ref, a_scr):
    @pl.when(pl.program_id(1) == 0)
    def _():
        a_scr[...] = (x_ref[...] * (1.0 + sc_ref[0]) + sh_ref[0]).astype(BF16)

    o_ref[...] = jnp.dot(a_scr[...], w_ref[...], preferred_element_type=F32)


def _proj(x2, sc, sh, w, seq):
    t, d = x2.shape
    n = w.shape[1]
    tm = min(1024, seq)
    tn = 512
    per_b = seq // tm
    return pl.pallas_call(
        _proj_kernel,
        grid=(t // tm, n // tn),
        in_specs=[pl.BlockSpec((tm, d), lambda i, j: (i, 0)),
                  pl.BlockSpec((1, 1, d), lambda i, j: (i // per_b, 0, 0)),
                  pl.BlockSpec((1, 1, d), lambda i, j: (i // per_b, 0, 0)),
                  pl.BlockSpec((d, tn), lambda i, j: (0, j))],
        out_specs=pl.BlockSpec((tm, tn), lambda i, j: (i, j)),
        out_shape=jax.ShapeDtypeStruct((t, n), F32),
        scratch_shapes=[pltpu.VMEM((tm, d), BF16)],
        compiler_params=_params(("parallel", "arbitrary")),
        name="proj",
    )(x2, sc, sh, w)


def _rglru_kernel(xr_ref, gr_ref, cw_ref, cb_ref, wa_ref, ba_ref, wx_ref, bx_ref, lam_ref,
                  o_ref, xbuf, abuf, bbuf, h_scr):
    ts, tc = xr_ref.shape
    pad = SUBLANES

    @pl.when(pl.program_id(2) == 0)
    def _():
        xbuf[0:pad, :] = jnp.zeros((pad, tc), F32)
        h_scr[...] = jnp.zeros_like(h_scr)

    xbuf[pad:pad + ts, :] = xr_ref[...]
    xc = cb_ref[...] + cw_ref[0:1, :] * xbuf[pad - 3:pad - 3 + ts, :]
    for k in range(1, CONV_WIDTH):
        off = pad - (CONV_WIDTH - 1) + k
        xc = xc + cw_ref[k:k + 1, :] * xbuf[off:off + ts, :]
    xbuf[0:pad, :] = xbuf[ts:ts + pad, :]

    xcb = xc.astype(BF16)
    ra, ia = [], []
    for hh in range(tc // LANES):
        xh = xcb[:, hh * LANES:(hh + 1) * LANES]
        ra.append(jnp.dot(xh, wa_ref[hh], preferred_element_type=F32))
        ia.append(jnp.dot(xh, wx_ref[hh], preferred_element_type=F32))
    r = jax.nn.sigmoid(jnp.concatenate(ra, axis=1) + ba_ref[...])
    ig = jax.nn.sigmoid(jnp.concatenate(ia, axis=1) + bx_ref[...])
    nlam = -lam_ref[...]
    softplus = jnp.maximum(nlam, 0.0) + jnp.log1p(jnp.exp(-jnp.abs(nlam)))
    log_a = (-LRU_C * r) * softplus
    a = jnp.exp(log_a)
    b = jnp.sqrt(-jnp.tanh(log_a) * (a * a + 1.0)) * (ig * xc)

    abuf[0:ts, :] = jnp.ones((ts, tc), F32)
    bbuf[0:ts, :] = jnp.zeros((ts, tc), F32)
    d = 1
    while d < ts:
        abuf[ts:2 * ts, :] = a
        bbuf[ts:2 * ts, :] = b
        a_sh = abuf[ts - d:2 * ts - d, :]
        b_sh = bbuf[ts - d:2 * ts - d, :]
        b = a * b_sh + b
        a = a * a_sh
        d *= 2
    h = a * h_scr[...] + b
    h_scr[...] = h[ts - 1:ts, :]
    o_ref[...] = (h * _gelu_tanh(gr_ref[...])).astype(o_ref.dtype)


def _rglru(proj3, col_xr, col_gr, conv_w, conv_b, wa, ba, wx, bx, lam):
    bsz, seq, _ = proj3.shape
    d_rnn = conv_w.shape[1]
    tc = min(512, d_rnn)
    ts = min(512, seq)
    nc = d_rnn // tc
    hpt = tc // LANES
    vec = lambda: pl.BlockSpec((1, tc), lambda b, c, s: (0, c))
    return pl.pallas_call(
        _rglru_kernel,
        grid=(bsz, nc, seq // ts),
        in_specs=[pl.BlockSpec((None, ts, tc), lambda b, c, s: (b, s, col_xr // tc + c)),
                  pl.BlockSpec((None, ts, tc), lambda b, c, s: (b, s, col_gr // tc + c)),
                  pl.BlockSpec((CONV_WIDTH, tc), lambda b, c, s: (0, c)),
                  vec(),
                  pl.BlockSpec((hpt, LANES, LANES), lambda b, c, s: (c, 0, 0)),
                  vec(),
                  pl.BlockSpec((hpt, LANES, LANES), lambda b, c, s: (c, 0, 0)),
                  vec(), vec()],
        out_specs=pl.BlockSpec((None, ts, tc), lambda b, c, s: (b, s, c)),
        out_shape=jax.ShapeDtypeStruct((bsz, seq, d_rnn), BF16),
        scratch_shapes=[pltpu.VMEM((ts + 2 * SUBLANES, tc), F32),
                        pltpu.VMEM((2 * ts, tc), F32),
                        pltpu.VMEM((2 * ts, tc), F32),
                        pltpu.VMEM((1, tc), F32)],
        compiler_params=_params(("parallel", "parallel", "arbitrary")),
        name="rglru",
    )(proj3, proj3, conv_w, conv_b, wa, ba, wx, bx, lam)


def _rope_kernel(pos_ref, invf_ref, cm_ref, sm_ref, cc_ref, ss_ref):
    ang = pos_ref[...].astype(F32) * invf_ref[...]
    cc_ref[...] = jnp.cos(ang) * cm_ref[...]
    ss_ref[...] = jnp.sin(ang) * sm_ref[...]


def _rope_tables(pos_col, invf, cmask, smask):
    t = pos_col.shape[0]
    tm = min(1024, t)
    row = lambda: pl.BlockSpec((1, LANES), lambda i: (0, 0))
    return pl.pallas_call(
        _rope_kernel,
        grid=(t // tm,),
        in_specs=[pl.BlockSpec((tm, 1), lambda i: (i, 0)), row(), row(), row()],
        out_specs=[pl.BlockSpec((tm, LANES), lambda i: (i, 0))] * 2,
        out_shape=[jax.ShapeDtypeStruct((t, LANES), F32)] * 2,
        compiler_params=_params(("parallel",)),
        name="rope",
    )(pos_col, invf, cmask, smask)


def _rms_bf16(x, g):
    return (x * lax.rsqrt(jnp.mean(x * x, axis=-1, keepdims=True) + RMS_EPS) * g).astype(BF16)


def _rope_lanes(r, cc, ss):
    return r * cc + pltpu.roll(r, LANES // 2, axis=1) * ss


PROJ_HEADS = 4


def _qproj_kernel(qc_ref, g_ref, w_ref, cc_ref, ss_ref, o_ref, a_scr):
    @pl.when(pl.program_id(1) == 0)
    def _():
        a_scr[...] = _rms_bf16(qc_ref[...], g_ref[...])

    qscale = (QK_NOPE + QK_ROPE) ** -0.5 * math.log2(math.e)
    a = a_scr[...]
    for hd in range(w_ref.shape[0]):
        acc = jnp.dot(a, w_ref[hd], preferred_element_type=F32)
        o_ref[hd, :, 0:QK_NOPE] = (acc[:, 0:QK_NOPE] * qscale).astype(o_ref.dtype)
        roped = _rope_lanes(acc[:, QK_NOPE:], cc_ref[...], ss_ref[...])
        o_ref[hd, :, QK_NOPE:] = (roped * qscale).astype(o_ref.dtype)


def _heads_per_step(nh):
    return PROJ_HEADS if nh % PROJ_HEADS == 0 else 1


def _qproj(proj, col_qc, g, w_ext, cc, ss, bsz, seq):
    t = proj.shape[0]
    nh, k, n = w_ext.shape
    hp = _heads_per_step(nh)
    tm = min(1024, seq)
    per_b = seq // tm
    return pl.pallas_call(
        _qproj_kernel,
        grid=(t // tm, nh // hp),
        in_specs=[pl.BlockSpec((tm, k), lambda i, h: (i, col_qc // k)),
                  pl.BlockSpec((1, k), lambda i, h: (0, 0)),
                  pl.BlockSpec((hp, k, n), lambda i, h: (h, 0, 0)),
                  pl.BlockSpec((tm, LANES), lambda i, h: (i, 0)),
                  pl.BlockSpec((tm, LANES), lambda i, h: (i, 0))],
        out_specs=pl.BlockSpec((None, hp, tm, n), lambda i, h: (i // per_b, h, i % per_b, 0)),
        out_shape=jax.ShapeDtypeStruct((bsz, nh, seq, n), BF16),
        scratch_shapes=[pltpu.VMEM((tm, k), BF16)],
        compiler_params=_params(("parallel", "arbitrary")),
        name="qproj",
    )(proj, g, w_ext, cc, ss)


def _kvproj_kernel(kvc_ref, kr_ref, g_ref, w_ref, cc_ref, ss_ref, k_ref, v_ref, a_scr, kr_scr):
    @pl.when(pl.program_id(1) == 0)
    def _():
        a_scr[...] = _rms_bf16(kvc_ref[...], g_ref[...])
        kr_scr[...] = _rope_lanes(kr_ref[...], cc_ref[...], ss_ref[...]).astype(kr_scr.dtype)

    a = a_scr[...]
    for hd in range(w_ref.shape[0]):
        acc = jnp.dot(a, w_ref[hd], preferred_element_type=F32)
        k_ref[hd, :, 0:QK_NOPE] = acc[:, 0:QK_NOPE].astype(k_ref.dtype)
        k_ref[hd, :, QK_NOPE:] = kr_scr[...]
        v_ref[hd] = acc[:, QK_NOPE:].astype(v_ref.dtype)


def _kvproj(proj, col_kvc, col_kr, g, w_h, cc, ss, bsz, seq):
    t = proj.shape[0]
    nh, k, n = w_h.shape
    hp = _heads_per_step(nh)
    tm = min(1024, seq)
    per_b = seq // tm
    omap = lambda i, h: (i // per_b, h, i % per_b, 0)
    return pl.pallas_call(
        _kvproj_kernel,
        grid=(t // tm, nh // hp),
        in_specs=[pl.BlockSpec((tm, k), lambda i, h: (i, col_kvc // k)),
                  pl.BlockSpec((tm, LANES), lambda i, h: (i, col_kr // LANES)),
                  pl.BlockSpec((1, k), lambda i, h: (0, 0)),
                  pl.BlockSpec((hp, k, n), lambda i, h: (h, 0, 0)),
                  pl.BlockSpec((tm, LANES), lambda i, h: (i, 0)),
                  pl.BlockSpec((tm, LANES), lambda i, h: (i, 0))],
        out_specs=[pl.BlockSpec((None, hp, tm, 2 * LANES), omap),
                   pl.BlockSpec((None, hp, tm, V_HEAD), omap)],
        out_shape=[jax.ShapeDtypeStruct((bsz, nh, seq, 2 * LANES), BF16),
                   jax.ShapeDtypeStruct((bsz, nh, seq, V_HEAD), BF16)],
        scratch_shapes=[pltpu.VMEM((tm, k), BF16), pltpu.VMEM((tm, LANES), BF16)],
        compiler_params=_params(("parallel", "arbitrary")),
        name="kvproj",
    )(proj, proj, g, w_h, cc, ss)


ATTN_ROWS = 32
ATTN_HEADS = 4


def _attn_kernel(qi_tab, ki_tab, q_ref, k_ref, v_ref, o_ref, m_scr, a_scr, acc_scr, s_scr, p_scr):
    step = pl.program_id(2)
    qi, ki = qi_tab[step], ki_tab[step]
    n_hd, tq, _ = q_ref.shape
    tk, dv = v_ref.shape[1], v_ref.shape[2]
    half = tq // 2
    n_chunks = tq // ATTN_ROWS

    @pl.when(ki == 0)
    def _():
        m_scr[...] = jnp.full_like(m_scr, NEG_INF)
        acc_scr[...] = jnp.zeros_like(acc_scr)

    def key_limit(hf, masked):
        return (hf + 1) * half if masked else tk

    def scores(hd, masked):
        for hf in range(2):
            hrows = slice(hf * half, (hf + 1) * half)
            klim = key_limit(hf, masked)
            s_scr[hd, hrows, 0:klim] = lax.dot_general(q_ref[hd, hrows, :], k_ref[hd, 0:klim, :],
                                                       (((1,), (1,)), ((), ())), preferred_element_type=F32)

    def softmax(hd, masked):
        def hidden(c, j):
            return masked and j * LANES > (c + 1) * ATTN_ROWS - 1

        def n_tiles(c):
            return key_limit(c * ATTN_ROWS // half, masked) // LANES

        for c in range(n_chunks):
            rows = slice(c * ATTN_ROWS, (c + 1) * ATTN_ROWS)
            tmax = None
            for j in range(n_tiles(c)):
                if hidden(c, j):
                    continue
                cols = slice(j * LANES, (j + 1) * LANES)
                s = s_scr[hd, rows, cols]
                if masked and (j + 1) * LANES - 1 > c * ATTN_ROWS:
                    row = c * ATTN_ROWS + lax.broadcasted_iota(jnp.int32, (ATTN_ROWS, LANES), 0)
                    col = j * LANES + lax.broadcasted_iota(jnp.int32, (ATTN_ROWS, LANES), 1)
                    s = jnp.where(col <= row, s, NEG_INF)
                    s_scr[hd, rows, cols] = s
                tmax = s if tmax is None else jnp.maximum(tmax, s)
            m_prev = m_scr[hd, rows, :]
            m_new = jnp.maximum(m_prev, jnp.max(tmax, axis=-1, keepdims=True))
            a_scr[hd, rows, :] = jnp.exp2(m_prev - m_new)
            m_scr[hd, rows, :] = m_new
        for c in range(n_chunks):
            rows = slice(c * ATTN_ROWS, (c + 1) * ATTN_ROWS)
            m_new = m_scr[hd, rows, :]
            for j in range(n_tiles(c)):
                cols = slice(j * LANES, (j + 1) * LANES)
                if hidden(c, j):
                    p_scr[hd, rows, cols] = jnp.zeros((ATTN_ROWS, LANES), BF16)
                else:
                    p_scr[hd, rows, cols] = jnp.exp2(s_scr[hd, rows, cols] - m_new).astype(BF16)

    def accumulate(hd, masked):
        v_ones = jnp.concatenate([v_ref[hd], jnp.ones((tk, dv), BF16)], axis=1)
        for hf in range(2):
            hrows = slice(hf * half, (hf + 1) * half)
            klim = key_limit(hf, masked)
            pv = jnp.dot(p_scr[hd, hrows, 0:klim], v_ones[0:klim, :], preferred_element_type=F32)
            alpha = a_scr[hd, hrows, :]
            acc_scr[hd, hrows, :] = (jnp.concatenate([alpha] * (2 * dv // LANES), axis=1) * acc_scr[hd, hrows, :]
                                     + pv)

    def block(masked):
        for hd in range(n_hd):
            scores(hd, masked)
        for hd in range(n_hd):
            softmax(hd, masked)
            accumulate(hd, masked)

    pl.when(ki < qi)(functools.partial(block, False))

    @pl.when(ki == qi)
    def _():
        block(True)
        for hd in range(n_hd):
            acc = acc_scr[hd]
            o_ref[:, hd * dv:(hd + 1) * dv] = (acc[:, :dv] / acc[:, dv:]).astype(o_ref.dtype)


def _attention(q, k, v):
    bsz, nh, seq, dk = q.shape
    dv = v.shape[-1]
    tq = tk = min(1024, seq)
    nq = seq // tq
    pairs = [(i, j) for i in range(nq) for j in range(i + 1)]
    qi_tab = jnp.asarray([p[0] for p in pairs], jnp.int32)
    ki_tab = jnp.asarray([p[1] for p in pairs], jnp.int32)
    hp = ATTN_HEADS if nh % ATTN_HEADS == 0 else 1
    grid_spec = pltpu.PrefetchScalarGridSpec(
        num_scalar_prefetch=2,
        grid=(bsz, nh // hp, len(pairs)),
        in_specs=[pl.BlockSpec((None, hp, tq, dk), lambda b, h, s, qt, kt: (b, h, qt[s], 0)),
                  pl.BlockSpec((None, hp, tk, dk), lambda b, h, s, qt, kt: (b, h, kt[s], 0)),
                  pl.BlockSpec((None, hp, tk, dv), lambda b, h, s, qt, kt: (b, h, kt[s], 0))],
        out_specs=pl.BlockSpec((None, tq, hp * dv), lambda b, h, s, qt, kt: (b, qt[s], h)),
        scratch_shapes=[pltpu.VMEM((hp, tq, LANES), F32), pltpu.VMEM((hp, tq, LANES), F32),
                        pltpu.VMEM((hp, tq, 2 * dv), F32), pltpu.VMEM((hp, tq, tk), F32),
                        pltpu.VMEM((hp, tq, tk), BF16)],
    )
    return pl.pallas_call(
        _attn_kernel,
        grid_spec=grid_spec,
        out_shape=jax.ShapeDtypeStruct((bsz, seq, nh * dv), BF16),
        compiler_params=_params(("parallel", "parallel", "arbitrary")),
        name="attn",
    )(qi_tab, ki_tab, q, k, v)


def _mix_kernel(hr_ref, o_ref, grnn_ref, gmla_ref, x_ref, gt_ref, w1_ref, w2_ref, wo_ref, lg_ref, lb_ref,
                out_ref, *, alpha):
    y_rnn = jnp.dot(hr_ref[...], w1_ref[...], preferred_element_type=F32)
    y_mla = jnp.dot(o_ref[...], w2_ref[...], preferred_element_type=F32)
    mixed = jax.nn.sigmoid(grnn_ref[...]) * y_rnn + jax.nn.sigmoid(gmla_ref[...]) * y_mla
    z = jnp.dot(mixed.astype(BF16), wo_ref[...], preferred_element_type=F32)
    z = alpha * x_ref[...] + (1.0 + gt_ref[0]) * z
    out_ref[...] = _layer_norm(z, lg_ref[...], lb_ref[...])


def _mix(hr, o, proj, col_grnn, col_gmla, x2, gt, w1, w2, wo, lg, lb, seq, alpha):
    t, d = x2.shape
    tm = min(256, seq)
    per_b = seq // tm
    rows = lambda width, cb=0: pl.BlockSpec((tm, width), lambda i: (i, cb))
    wspec = lambda w: _resident(w.shape, lambda i: (0, 0))
    return pl.pallas_call(
        functools.partial(_mix_kernel, alpha=alpha),
        grid=(t // tm,),
        in_specs=[rows(hr.shape[1]), rows(o.shape[1]),
                  rows(d, col_grnn // d), rows(d, col_gmla // d), rows(d),
                  pl.BlockSpec((1, 1, d), lambda i: (i // per_b, 0, 0)),
                  wspec(w1), wspec(w2), wspec(wo),
                  _resident((1, d), lambda i: (0, 0)), _resident((1, d), lambda i: (0, 0))],
        out_specs=rows(d),
        out_shape=jax.ShapeDtypeStruct((t, d), F32),
        compiler_params=_params(("parallel",)),
        name="mix",
    )(hr, o, proj, proj, x2, gt, w1, w2, wo, lg, lb)


def _topk_rows(s, k):
    n, tm = s.shape
    iota = lax.broadcasted_iota(jnp.int32, (n, tm), 0).astype(F32)
    krow = lax.broadcasted_iota(jnp.int32, (k, tm), 0)
    ts = jnp.zeros((k, tm), F32)
    ti = jnp.zeros((k, tm), F32)
    for kk in range(k):
        m = jnp.max(s, axis=0, keepdims=True)
        idx = jnp.min(jnp.where(s == m, iota, float(n)), axis=0, keepdims=True)
        ts = jnp.where(krow == kk, m, ts)
        ti = jnp.where(krow == kk, idx, ti)
        s = jnp.where(iota == idx, NEG_INF, s)
    return ts, ti


def _pair_candidates(l1, l2, combine):
    assert l1.shape[0] == 16 and SUBLANES == 8
    row = lax.broadcasted_iota(jnp.int32, (SUBLANES, l1.shape[1]), 0)
    r1 = lambda i: l1[i:i + 1, :]
    lo2 = l2[0:8, :]
    sh2 = lambda d: pltpu.roll(lo2, d, axis=0)
    vregs = [
        (combine(r1(0), lo2), [(0, j) for j in range(8)]),
        (combine(r1(0), l2[8:16, :]), [(0, j) for j in range(8, 16)]),
        (combine(r1(1), lo2), [(1, j) for j in range(8)]),
        (combine(l1[8:16, :], l2[0:1, :]), [(i, 0) for i in range(8, 16)]),
        (combine(jnp.where(row < 5, r1(2), r1(4)), jnp.where(row < 5, lo2, sh2(5))),
         [(2, j) for j in range(5)] + [(4, j) for j in range(3)]),
        (combine(jnp.where(row < 4, r1(3), jnp.where(row < 6, r1(5), r1(6))),
                 jnp.where(row < 4, lo2, jnp.where(row < 6, sh2(4), sh2(6)))),
         [(3, j) for j in range(4)] + [(5, j) for j in range(2)] + [(6, j) for j in range(2)]),
        (combine(r1(7), lo2), [(7, 0), (7, 1)] + [None] * 6),
    ]
    return vregs


def _pair_topk(ts1, ti1, ts2, ti2, n_keys):
    k, tm = ts1.shape
    row = lax.broadcasted_iota(jnp.int32, (SUBLANES, tm), 0)
    rowk = lax.broadcasted_iota(jnp.int32, (k, tm), 0)
    sums = _pair_candidates(ts1, ts2, lambda a, b: a + b)
    experts = _pair_candidates(ti1, ti2, lambda a, b: a * float(n_keys) + b)
    cs, fs, es = [], [], []
    for (c, table), (e, _) in zip(sums, experts):
        flat = jnp.full((SUBLANES, tm), -1.0, F32)
        for r, ij in enumerate(table):
            if ij is None:
                c = jnp.where(row == r, NEG_INF, c)
            else:
                flat = jnp.where(row == r, float(ij[0] * k + ij[1]), flat)
        cs.append(c)
        fs.append(flat)
        es.append(e)
    cand = jnp.concatenate(cs, axis=0)
    flat = jnp.concatenate(fs, axis=0)
    expert = jnp.concatenate(es, axis=0)
    best_s = jnp.zeros((k, tm), F32)
    best_e = jnp.zeros((k, tm), F32)
    for kk in range(k):
        m = jnp.max(cand, axis=0, keepdims=True)
        fi = jnp.min(jnp.where(cand == m, flat, float(k * k)), axis=0, keepdims=True)
        sel = flat == fi
        ev = jnp.sum(jnp.where(sel, expert, 0.0), axis=0, keepdims=True)
        best_s = jnp.where(rowk == kk, m, best_s)
        best_e = jnp.where(rowk == kk, ev, best_e)
        cand = jnp.where(sel, NEG_INF, cand)
    return best_s, best_e


def _route_kernel(x_ref, sc_ref, sh_ref, wq_ref, keys_ref, h2_ref, e_ref, g_ref, q_scr, e_scr, g_scr,
                  *, n_heads):
    n_keys = keys_ref.shape[1]
    dk = keys_ref.shape[2]
    h2 = (x_ref[...] * (1.0 + sc_ref[0]) + sh_ref[0]).astype(BF16)
    h2_ref[...] = h2
    q = jnp.dot(h2, wq_ref[...], preferred_element_type=F32)
    for c in range(2 * n_heads):
        q_scr[c] = q[:, c * dk:(c + 1) * dk].astype(BF16)

    def head_body(h, carry):
        tops = []
        for p in range(2):
            s = lax.dot_general(keys_ref[p], q_scr[2 * h + p], (((1,), (1,)), ((), ())),
                                preferred_element_type=F32)
            tops.extend(_topk_rows(s, PEER_TOPK))
        best_s, best_e = _pair_topk(*tops, n_keys)
        ex = jnp.exp(best_s - jnp.max(best_s, axis=0, keepdims=True))
        gate = ex / jnp.sum(ex, axis=0, keepdims=True)
        r0 = pl.multiple_of(h * PEER_TOPK, PEER_TOPK)
        e_scr[pl.ds(r0, PEER_TOPK), :] = best_e
        g_scr[pl.ds(r0, PEER_TOPK), :] = gate
        return carry

    lax.fori_loop(0, n_heads, head_body, 0)
    e_ref[...] = e_scr[...].T
    g_ref[...] = g_scr[...].T


def _route(x1, sc, sh, wq, keys, seq):
    t, d = x1.shape
    nq = wq.shape[1]
    dk = keys.shape[2]
    n_heads = nq // (2 * dk)
    slots = n_heads * PEER_TOPK
    tm = min(512, seq)
    per_b = seq // tm
    return pl.pallas_call(
        functools.partial(_route_kernel, n_heads=n_heads),
        grid=(t // tm,),
        in_specs=[pl.BlockSpec((tm, d), lambda i: (i, 0)),
                  pl.BlockSpec((1, 1, d), lambda i: (i // per_b, 0, 0)),
                  pl.BlockSpec((1, 1, d), lambda i: (i // per_b, 0, 0)),
                  _resident(wq.shape, lambda i: (0, 0)),
                  _resident(keys.shape, lambda i: (0, 0, 0))],
        out_specs=[pl.BlockSpec((tm, d), lambda i: (i, 0)),
                   pl.BlockSpec((tm, slots), lambda i: (i, 0)),
                   pl.BlockSpec((tm, slots), lambda i: (i, 0))],
        out_shape=[jax.ShapeDtypeStruct((t, d), BF16),
                   jax.ShapeDtypeStruct((t, slots), F32),
                   jax.ShapeDtypeStruct((t, slots), F32)],
        scratch_shapes=[pltpu.VMEM((2 * n_heads, tm, dk), BF16),
                        pltpu.VMEM((slots, tm), F32),
                        pltpu.VMEM((slots, tm), F32)],
        compiler_params=_params(("parallel",)),
        name="route",
    )(x1, sc, sh, wq, keys)


GB_TOKENS = 128
GB_PITCH = GB_TOKENS + SUBLANES
GB_UNROLL = 64


def _gbuild_kernel(e_ref, g_ref, o_ref, *, n_keys):
    tmb, slots = e_ref.shape
    sub = lax.broadcasted_iota(jnp.int32, (n_keys, slots), 0).astype(F32)
    zeros = jnp.zeros((n_keys, n_keys), F32)
    for r in range(tmb, GB_PITCH):
        o_ref[pl.ds(r, n_keys, stride=GB_PITCH), :] = zeros

    def one_hots(t):
        e = e_ref[pl.ds(t, 1), :]
        g = g_ref[pl.ds(t, 1), :]
        a = jnp.floor(e * (1.0 / n_keys))
        b = e - a * float(n_keys)
        p1 = jnp.where(sub == a, g, 0.0).astype(BF16)
        p2 = jnp.where(sub == b, 1.0, 0.0).astype(BF16)
        return p1, p2

    zpad = jnp.zeros((n_keys, slots), BF16)

    def body(tb, carry):
        for u in range(0, GB_UNROLL, 2):
            ta = tb * GB_UNROLL + u
            p1a, p2a = one_hots(ta)
            p1b, p2b = one_hots(ta + 1)
            lhs = jnp.concatenate([p1a, p1b], axis=1)
            rhs = jnp.concatenate([jnp.concatenate([p2a, zpad], axis=1),
                                   jnp.concatenate([zpad, p2b], axis=1)], axis=0)
            grids = lax.dot_general(lhs, rhs, (((1,), (1,)), ((), ())), preferred_element_type=F32)
            o_ref[pl.ds(ta, n_keys, stride=GB_PITCH), :] = grids[:, 0:n_keys]
            o_ref[pl.ds(ta + 1, n_keys, stride=GB_PITCH), :] = grids[:, n_keys:]
        return carry

    lax.fori_loop(0, tmb // GB_UNROLL, body, 0)


def _gbuild(e, g, n_keys):
    t, slots = e.shape
    tmb = GB_TOKENS
    return pl.pallas_call(
        functools.partial(_gbuild_kernel, n_keys=n_keys),
        grid=(t // tmb,),
        in_specs=[pl.BlockSpec((tmb, slots), lambda i: (i, 0))] * 2,
        out_specs=pl.BlockSpec((None, n_keys * GB_PITCH, n_keys), lambda i: (i, 0, 0)),
        out_shape=jax.ShapeDtypeStruct((t // tmb, n_keys * GB_PITCH, n_keys), F32),
        compiler_params=_params(("parallel",)),
        name="gbuild",
    )(e, g)


def _peer_kernel(h2_ref, ut_ref, v_ref, g_ref, x1_ref, gt_ref, lg_ref, lb_ref, o_ref, coef_scr, *, alpha):
    j = pl.program_id(1)
    n_tiles = pl.num_programs(1) - 1
    _, n1, _, n_keys = g_ref.shape
    tm = h2_ref.shape[0]

    def score_tile():
        a = jnp.dot(h2_ref[...], ut_ref[...], preferred_element_type=F32)
        slot = j % 2
        for k in range(n1):
            cols = slice(k * n_keys, (k + 1) * n_keys)
            gate = g_ref[:, k, 0:GB_TOKENS, :].reshape(tm, n_keys)
            coef_scr[slot, :, cols] = (_gelu_tanh(a[:, cols]) * gate).astype(BF16)

    def accumulate():
        o_ref[...] += jnp.dot(coef_scr[(j + 1) % 2], v_ref[...], preferred_element_type=F32)

    @pl.when(j == 0)
    def _():
        o_ref[...] = jnp.zeros_like(o_ref)
        score_tile()

    @pl.when(jnp.logical_and(j > 0, j < n_tiles))
    def _():
        accumulate()
        score_tile()

    @pl.when(j == n_tiles)
    def _():
        accumulate()
        z = alpha * x1_ref[...] + (1.0 + gt_ref[0]) * o_ref[...]
        o_ref[...] = _layer_norm(z, lg_ref[...], lb_ref[...])


def _peer(h2, ut, v, g4, x1, gt, lg, lb, seq, alpha):
    t, d = h2.shape
    ne = v.shape[0]
    n_keys = g4.shape[3]
    tm = min(512, seq)
    te = 1024
    n1 = te // n_keys
    n_tiles = ne // te
    per_b = seq // tm
    cur = lambda j: jnp.minimum(j, n_tiles - 1)
    prev = lambda j: jnp.maximum(j - 1, 0)
    return pl.pallas_call(
        functools.partial(_peer_kernel, alpha=alpha),
        grid=(t // tm, n_tiles + 1),
        in_specs=[pl.BlockSpec((tm, d), lambda i, j: (i, 0)),
                  pl.BlockSpec((d, te), lambda i, j: (0, cur(j))),
                  pl.BlockSpec((te, d), lambda i, j: (prev(j), 0)),
                  pl.BlockSpec((tm // GB_TOKENS, n1, GB_PITCH, n_keys), lambda i, j: (i, cur(j), 0, 0)),
                  pl.BlockSpec((tm, d), lambda i, j: (i, 0)),
                  pl.BlockSpec((1, 1, d), lambda i, j: (i // per_b, 0, 0)),
                  pl.BlockSpec((1, d), lambda i, j: (0, 0)),
                  pl.BlockSpec((1, d), lambda i, j: (0, 0))],
        out_specs=pl.BlockSpec((tm, d), lambda i, j: (i, 0)),
        out_shape=jax.ShapeDtypeStruct((t, d), F32),
        scratch_shapes=[pltpu.VMEM((2, tm, te), BF16)],
        compiler_params=_params(("parallel", "arbitrary")),
        name="peer",
    )(h2, ut, v, g4, x1, gt, lg, lb)


def _swap_halves(w):
    half = w.shape[-1] // 2
    return jnp.concatenate([w[..., half:], w[..., :half]], axis=-1)


def _layer(x2, mod, cc, ss, bsz, seq, depth, p):
    t, d = x2.shape
    sh1, sc1, gt1, sh2, sc2, gt2 = [m[:, None, :] for m in jnp.split(mod, 6, axis=-1)]
    alpha = (2.0 * depth) ** 0.25
    d_rnn = p["conv_w"].shape[1]
    q_lora = p["q_norm_g"].shape[0]
    kv_lora = p["kv_norm_g"].shape[0]
    n_heads = p["w_uq"].shape[1] // (QK_NOPE + QK_ROPE)

    w_in = p["w_in"]
    widths = (d_rnn, d_rnn, q_lora, kv_lora, QK_ROPE, d, d)
    offs = [0]
    for wd in widths:
        offs.append(offs[-1] + wd)
    pieces = [w_in[:, offs[i]:offs[i + 1]] for i in range(7)]
    w_xr, w_gr, w_qc, w_kvc, w_kr, w_grnn, w_gmla = pieces
    kr_pad = 512
    w_kr_ext = jnp.concatenate([w_kr, _swap_halves(w_kr),
                                jnp.zeros((d, kr_pad - 2 * QK_ROPE), w_in.dtype)], axis=1)
    w_in_p = jnp.concatenate([w_xr, w_gr, w_grnn, w_gmla, w_qc, w_kvc, w_kr_ext], axis=1).astype(BF16)
    col_xr, col_gr, col_grnn, col_gmla = 0, d_rnn, 2 * d_rnn, 2 * d_rnn + d
    col_qc = 2 * d_rnn + 2 * d
    col_kvc = col_qc + q_lora
    col_kr = col_kvc + kv_lora

    proj = _proj(x2, sc1, sh1, w_in_p, seq)
    proj3 = proj.reshape(bsz, seq, proj.shape[1])

    row = lambda v: v[None, :]
    hr = _rglru(proj3, col_xr, col_gr, p["conv_w"], row(p["conv_b"]),
                p["w_rg_a"].astype(BF16), row(p["b_rg_a"]), p["w_rg_x"].astype(BF16), row(p["b_rg_x"]),
                row(p["rg_lambda"]))
    hr = hr.reshape(t, d_rnn)

    w_uq = p["w_uq"].reshape(q_lora, n_heads, QK_NOPE + QK_ROPE)
    w_rope = w_uq[..., QK_NOPE:]
    w_uq_ext = jnp.concatenate([w_uq[..., :QK_NOPE], w_rope, _swap_halves(w_rope)], axis=-1)
    w_uq_ext = w_uq_ext.transpose(1, 0, 2).astype(BF16)
    w_ukv = p["w_ukv"].reshape(kv_lora, n_heads, QK_NOPE + V_HEAD).transpose(1, 0, 2).astype(BF16)
    q = _qproj(proj, col_qc, row(p["q_norm_g"]), w_uq_ext, cc, ss, bsz, seq)
    k, v = _kvproj(proj, col_kvc, col_kr, row(p["kv_norm_g"]), w_ukv, cc, ss, bsz, seq)
    o = _attention(q, k, v)
    o = o.reshape(t, n_heads * V_HEAD)

    x1 = _mix(hr, o, proj, col_grnn, col_gmla, x2, gt1,
              p["w_rnn_out"].astype(BF16), p["w_mla_out"].astype(BF16), p["w_o"].astype(BF16),
              row(p["ln1_g"]), row(p["ln1_b"]), seq, alpha)

    keys = p["peer_keys"]
    n_keys = keys.shape[1]
    h2, e, g = _route(x1, sc2, sh2, p["peer_wq"].astype(BF16), keys.astype(BF16), seq)
    g3 = _gbuild(e, g, n_keys)
    g4 = g3.reshape(g3.shape[0], n_keys, GB_PITCH, n_keys)
    return _peer(h2, p["peer_u"].astype(BF16).T, p["peer_v"].astype(BF16), g4, x1, gt2,
                 row(p["ln2_g"]), row(p["ln2_b"]), seq, alpha)


def kernel(x, c, positions, w_ada, b_ada, w_in, conv_w, conv_b, w_rg_a, b_rg_a, w_rg_x, b_rg_x, rg_lambda,
           w_rnn_out, q_norm_g, w_uq, kv_norm_g, w_ukv, w_mla_out, w_o, ln1_g, ln1_b, peer_wq, peer_keys,
           peer_u, peer_v, ln2_g, ln2_b):
    bsz, seq, d = x.shape
    depth = w_ada.shape[0]
    t = bsz * seq
    stacked = dict(w_in=w_in, conv_w=conv_w, conv_b=conv_b, w_rg_a=w_rg_a, b_rg_a=b_rg_a, w_rg_x=w_rg_x,
                   b_rg_x=b_rg_x, rg_lambda=rg_lambda, w_rnn_out=w_rnn_out, q_norm_g=q_norm_g, w_uq=w_uq,
                   kv_norm_g=kv_norm_g, w_ukv=w_ukv, w_mla_out=w_mla_out, w_o=w_o, ln1_g=ln1_g, ln1_b=ln1_b,
                   peer_wq=peer_wq, peer_keys=peer_keys, peer_u=peer_u, peer_v=peer_v, ln2_g=ln2_g,
                   ln2_b=ln2_b)

    half = QK_ROPE // 2
    inv_freq = ROPE_THETA ** (-jnp.arange(0, QK_ROPE, 2, dtype=F32) / QK_ROPE)
    zeros = jnp.zeros((half,), F32)
    ones = jnp.ones((half,), F32)
    invf = jnp.concatenate([inv_freq, inv_freq, zeros, zeros])[None, :]
    cmask = jnp.concatenate([ones, ones, zeros, zeros])[None, :]
    smask = jnp.concatenate([-ones, ones, zeros, zeros])[None, :]
    cc, ss = _rope_tables(positions.reshape(t, 1), invf, cmask, smask)

    c_pad = jnp.concatenate([c, jnp.zeros((SUBLANES - bsz % SUBLANES, d), c.dtype)], axis=0)
    x2 = x.reshape(t, d)
    for l in range(depth):
        mod = _ada(c_pad, w_ada[l], b_ada[l][None, :])[:bsz]
        x2 = _layer(x2, mod, cc, ss, bsz, seq, depth, {k: v[l] for k, v in stacked.items()})
    return x2.reshape(bsz, seq, d)
```
